```python
import math
import jax
import jax.numpy as jnp
from jax import lax
import numpy as np

D_MODEL = 2048
BATCH = 8
SEQ = 2048
DEPTH = 4

GRID_W = 64
MIX_W = D_MODEL * 3 // 8
NA_HEAD_DIM = 64
NA_HEADS = MIX_W // NA_HEAD_DIM
NA_W = NA_HEADS * NA_HEAD_DIM
NA_KH_MAX = 8
NA_KW = 16
NA_QB = 16
NA_KB = 32
HY_W = MIX_W
HY_ORDER = 2
HY_SHORT_K = 3
HY_POS_DIM = 33
HY_FILT_HID = 64
HY_FAST_DECAY = 0.3
HY_SLOW_DECAY = 1.5
HY_DECAY_TARGET = 1e-2
CF_W = MIX_W
CF_K = 31
N_GROUPS = 4
EXPERTS_PER_GROUP = 4
N_EXPERTS = N_GROUPS * EXPERTS_PER_GROUP
TOP_K = 2
D_EXPERT = D_MODEL // 2
C_IN = 3 * NA_W + 3 * HY_W + 2 * CF_W + 3 * D_MODEL
ALPHA = (2 * DEPTH) ** 0.25
BETA = (8 * DEPTH) ** -0.25
LN_EPS = 1e-5
NEG_INF = -1e30

kernel_name = 'hybrid_natten_hyena_conformer_moe_encoder'


def layer_norm(x, g, b):
    xf = x.astype(jnp.float32)
    mu = jnp.mean(xf, axis=-1, keepdims=True)
    xc = xf - mu
    var = jnp.mean(xc * xc, axis=-1, keepdims=True)
    y = xc * lax.rsqrt(var + LN_EPS) * g.astype(jnp.float32) + b.astype(jnp.float32)
    return y.astype(x.dtype)


def depthwise_conv_same(u, w, b):
    k = w.shape[0]
    pad = k // 2
    out = lax.conv_general_dilated(u, w[:, None, :].astype(u.dtype), window_strides=(1,),
                                   padding=[(pad, pad)],
                                   dimension_numbers=('NWC', 'WIO', 'NWC'),
                                   feature_group_count=u.shape[-1])
    return out + b.astype(u.dtype)


def neighbourhood_attention(q, k, v, rpb):
    B, L, _ = q.shape
    rows = L // GRID_W
    kh = min(NA_KH_MAX, rows)

    def to_grid(t):
        return t.reshape(B, rows, GRID_W, NA_HEADS, NA_HEAD_DIM).transpose(0, 3, 1, 2, 4)

    qg = to_grid(q * (NA_HEAD_DIM ** -0.5))
    kg = to_grid(k)
    vg = to_grid(v)

    n_cb = GRID_W // NA_QB
    qc = np.arange(GRID_W).reshape(n_cb, NA_QB)
    band0 = np.clip(np.arange(n_cb) * NA_QB - NA_KW // 2, 0, GRID_W - NA_KB)
    kc = band0[:, None] + np.arange(NA_KB)
    cs = np.clip(qc - NA_KW // 2, 0, GRID_W - NA_KW)
    col_valid = (kc[:, None, :] >= cs[:, :, None]) & (kc[:, None, :] < cs[:, :, None] + NA_KW)
    col_off = np.clip(kc[:, None, :] - qc[:, :, None], -(NA_KW - 1), NA_KW - 1) + (NA_KW - 1)
    col_valid = jnp.asarray(col_valid)
    col_off = jnp.asarray(col_off.astype(np.int32))
    kc_flat = jnp.asarray(kc.reshape(-1).astype(np.int32))
    rpb32 = rpb.astype(jnp.float32)

    def row_block(r):
        rs = jnp.clip(r - kh // 2, 0, rows - kh)
        k_rows = lax.dynamic_slice_in_dim(kg, rs, kh, axis=2)
        v_rows = lax.dynamic_slice_in_dim(vg, rs, kh, axis=2)
        k_band = jnp.take(k_rows, kc_flat, axis=3).reshape(B, NA_HEADS, kh, n_cb, NA_KB, NA_HEAD_DIM)
        v_band = jnp.take(v_rows, kc_flat, axis=3).reshape(B, NA_HEADS, kh, n_cb, NA_KB, NA_HEAD_DIM)
        q_row = lax.dynamic_index_in_dim(qg, r, axis=2, keepdims=False)
        q_row = q_row.reshape(B, NA_HEADS, n_cb, NA_QB, NA_HEAD_DIM)
        s = jnp.einsum('bhcqd,bhkcjd->bhcqkj', q_row, k_band).astype(jnp.float32)
        row_off = rs + jnp.arange(kh) - r + (NA_KH_MAX - 1)
        bias = jnp.take(rpb32, row_off, axis=1)
        bias = bias[:, :, col_off].transpose(0, 2, 3, 1, 4)
        bias = jnp.where(col_valid[None, :, :, None, :], bias, NEG_INF)
        s = s + bias[None]
        p = jax.nn.softmax(s.reshape(B, NA_HEADS, n_cb, NA_QB, kh * NA_KB), axis=-1)
        p = p.reshape(s.shape).astype(v.dtype)
        o = jnp.einsum('bhcqkj,bhkcjd->bhcqd', p, v_band)
        return o.reshape(B, NA_HEADS, GRID_W, NA_HEAD_DIM)

    out = lax.map(row_block, jnp.arange(rows))
    return out.transpose(1, 0, 3, 2, 4).reshape(B, L, NA_W)


def hyena_filter_spectra(L, w1, b1, w2, b2, w3, b3, freq, w4):
    f32 = jnp.float32
    t = jnp.linspace(0.0, 1.0, L, dtype=f32)[:, None]
    bands = (HY_POS_DIM - 1) // 2
    w = 2.0 * math.pi * jnp.arange(L, dtype=f32)[:, None] / L
    f = jnp.linspace(1e-4, bands - 1, bands, dtype=f32)[None, :]
    z = jnp.concatenate([t, jnp.cos(f * w), -jnp.sin(f * w)], axis=-1)
    fr = freq.astype(f32)
    h = jnp.sin(fr * (z @ w1.astype(f32) + b1.astype(f32)))
    h = jnp.sin(fr * (h @ w2.astype(f32) + b2.astype(f32)))
    h = jnp.sin(fr * (h @ w3.astype(f32) + b3.astype(f32)))
    h = h @ w4.astype(f32)
    max_decay = math.log(HY_DECAY_TARGET) / HY_FAST_DECAY
    min_decay = math.log(HY_DECAY_TARGET) / HY_SLOW_DECAY
    deltas = jnp.abs(jnp.linspace(min_decay, max_decay, h.shape[-1], dtype=f32))
    h = h * jnp.exp(-t * deltas)
    h = h.reshape(L, 2, HY_ORDER, HY_W)
    filt = jnp.concatenate([h[:, 0], jnp.zeros((1, HY_ORDER, HY_W), f32), h[:0:-1, 1]], axis=0)
    return jnp.fft.rfft(filt, axis=0)


def hyena_mixer(u, conv_w, conv_b, w1, b1, w2, b2, w3, b3, freq, w4, skip):
    B, L, _ = u.shape
    uc = depthwise_conv_same(u, conv_w, conv_b).astype(jnp.float32)
    vv, x1, x2 = jnp.split(uc, 3, axis=-1)
    spec = hyena_filter_spectra(L, w1, b1, w2, b2, w3, b3, freq, w4)
    skip32 = skip.astype(jnp.float32)

    def long_conv(z, o):
        zf = jnp.fft.rfft(z, n=2 * L, axis=1)
        y = jnp.fft.irfft(zf * spec[None, :, o, :], n=2 * L, axis=1)[:, :L]
        return y + z * skip32[o]

    z = x1 * long_conv(vv, 0)
    z = x2 * long_conv(z, 1)
    return z.astype(u.dtype)


def conformer_conv(u, dw_w, dw_b, ln_g, ln_b):
    a, g = jnp.split(u, 2, axis=-1)
    z = a * jax.nn.sigmoid(g)
    z = depthwise_conv_same(z, dw_w, dw_b)
    z = layer_norm(z, ln_g, ln_b)
    return jax.nn.silu(z)


def grouped_moe(x, w_router, b_router, w_gate, w_up, w_down):
    B, L, D = x.shape
    xt = x.reshape(B * L, D)
    logits = (xt @ w_router + b_router).astype(jnp.float32)
    probs = jax.nn.softmax(logits, axis=-1).reshape(-1, N_GROUPS, EXPERTS_PER_GROUP)
    group_score = lax.top_k(probs, TOP_K)[0].sum(-1)
    g_sel = jnp.argmax(group_score, axis=-1)
    p_grp = jnp.take_along_axis(probs, g_sel[:, None, None], axis=1)[:, 0]
    top_p, top_i = lax.top_k(p_grp, TOP_K)
    top_w = top_p / jnp.sum(top_p, axis=-1, keepdims=True)
    expert_id = g_sel[:, None] * EXPERTS_PER_GROUP + top_i
    combine = jnp.sum(jax.nn.one_hot(expert_id, N_EXPERTS, dtype=jnp.float32) * top_w[..., None], axis=1)
    combine = combine.astype(x.dtype)
    y = jnp.zeros_like(xt)
    for e in range(N_EXPERTS):
        h = jax.nn.silu(xt @ w_gate[e]) * (xt @ w_up[e])
        y = y + combine[:, e:e + 1] * (h @ w_down[e])
    return y.reshape(B, L, D)


def setup_inputs(seed: int = 0) -> dict:
    key = jax.random.key(seed)
    ks = jax.random.split(key, 35)
    f32 = jnp.float32

    def nrm(i, shape, std):
        return jax.random.normal(ks[i], shape, f32) * std

    col_scale = np.ones((C_IN,), np.float32)
    col_scale[2 * NA_W:3 * NA_W] = BETA
    return {
        'x': nrm(0, (BATCH, SEQ, D_MODEL), 1.0),
        'in_ln_g': 1.0 + nrm(1, (D_MODEL,), 0.01),
        'in_ln_b': nrm(2, (D_MODEL,), 0.01),
        'w_in': nrm(3, (DEPTH, D_MODEL, C_IN), D_MODEL ** -0.5) * jnp.asarray(col_scale),
        'b_in': nrm(4, (DEPTH, C_IN), 0.01),
        'attn_rpb': nrm(5, (DEPTH, NA_HEADS, 2 * NA_KH_MAX - 1, 2 * NA_KW - 1), 0.02),
        'hy_conv_w': nrm(6, (DEPTH, HY_SHORT_K, 3 * HY_W), HY_SHORT_K ** -0.5),
        'hy_conv_b': nrm(7, (DEPTH, 3 * HY_W), 0.01),
        'hy_f_w1': nrm(8, (DEPTH, HY_POS_DIM, HY_FILT_HID), HY_POS_DIM ** -0.5),
        'hy_f_b1': nrm(9, (DEPTH, HY_FILT_HID), 0.1),
        'hy_f_w2': nrm(10, (DEPTH, HY_FILT_HID, HY_FILT_HID), HY_FILT_HID ** -0.5),
        'hy_f_b2': nrm(11, (DEPTH, HY_FILT_HID), 0.1),
        'hy_f_w3': nrm(12, (DEPTH, HY_FILT_HID, HY_FILT_HID), HY_FILT_HID ** -0.5),
        'hy_f_b3': nrm(13, (DEPTH, HY_FILT_HID), 0.1),
        'hy_f_freq': 1.0 + nrm(14, (DEPTH, HY_FILT_HID), 0.01),
        'hy_f_w4': nrm(15, (DEPTH, HY_FILT_HID, 2 * HY_ORDER * HY_W), 0.01),
        'hy_skip': nrm(16, (DEPTH, HY_ORDER, HY_W), 0.5),
        'cf_dw_w': nrm(17, (DEPTH, CF_K, CF_W), CF_K ** -0.5),
        'cf_dw_b': nrm(18, (DEPTH, CF_W), 0.01),
        'cf_ln_g': 1.0 + nrm(19, (DEPTH, CF_W), 0.01),
        'cf_ln_b': nrm(20, (DEPTH, CF_W), 0.01),
        'w_attn_br': nrm(21, (DEPTH, NA_W, D_MODEL), NA_W ** -0.5 * BETA),
        'w_hy_br': nrm(22, (DEPTH, HY_W, D_MODEL), HY_W ** -0.5 * BETA),
        'w_cf_br': nrm(23, (DEPTH, CF_W, D_MODEL), CF_W ** -0.5 * BETA),
        'w_o': nrm(24, (DEPTH, D_MODEL, D_MODEL), D_MODEL ** -0.5 * BETA),
        'b_o': nrm(25, (DEPTH, D_MODEL), 0.01),
        'ln1_g': 1.0 + nrm(26, (DEPTH, D_MODEL), 0.01),
        'ln1_b': nrm(27, (DEPTH, D_MODEL), 0.01),
        'w_router': nrm(28, (D_MODEL, N_EXPERTS), D_MODEL ** -0.5),
        'b_router': nrm(29, (N_EXPERTS,), 0.01),
        'moe_w_gate': nrm(30, (DEPTH, N_EXPERTS, D_MODEL, D_EXPERT), D_MODEL ** -0.5),
        'moe_w_up': nrm(31, (DEPTH, N_EXPERTS, D_MODEL, D_EXPERT), D_MODEL ** -0.5 * BETA),
        'moe_w_down': nrm(32, (DEPTH, N_EXPERTS, D_EXPERT, D_MODEL), D_EXPERT ** -0.5 * BETA),
        'ln2_g': 1.0 + nrm(33, (DEPTH, D_MODEL), 0.01),
        'ln2_b': nrm(34, (DEPTH, D_MODEL), 0.01),
    }


def reference(x, in_ln_g, in_ln_b, w_in, b_in, attn_rpb, hy_conv_w, hy_conv_b,
              hy_f_w1, hy_f_b1, hy_f_w2, hy_f_b2, hy_f_w3, hy_f_b3, hy_f_freq, hy_f_w4,
              hy_skip, cf_dw_w, cf_dw_b, cf_ln_g, cf_ln_b, w_attn_br, w_hy_br, w_cf_br,
              w_o, b_o, ln1_g, ln1_b, w_router, b_router, moe_w_gate, moe_w_up,
              moe_w_down, ln2_g, ln2_b):
    split_pts = [NA_W, 2 * NA_W, 3 * NA_W, 3 * NA_W + 3 * HY_W,
                 3 * NA_W + 3 * HY_W + 2 * CF_W,
                 3 * NA_W + 3 * HY_W + 2 * CF_W + D_MODEL,
                 3 * NA_W + 3 * HY_W + 2 * CF_W + 2 * D_MODEL]
    h = layer_norm(x, in_ln_g, in_ln_b)
    for l in range(DEPTH):
        p = h @ w_in[l] + b_in[l]
        q, k, v, hy_in, cf_in, g_a, g_h, g_c = jnp.split(p, split_pts, axis=-1)
        y_a = neighbourhood_attention(q, k, v, attn_rpb[l])
        y_h = hyena_mixer(hy_in, hy_conv_w[l], hy_conv_b[l], hy_f_w1[l], hy_f_b1[l],
                          hy_f_w2[l], hy_f_b2[l], hy_f_w3[l], hy_f_b3[l], hy_f_freq[l],
                          hy_f_w4[l], hy_skip[l])
        y_c = conformer_conv(cf_in, cf_dw_w[l], cf_dw_b[l], cf_ln_g[l], cf_ln_b[l])
        merged = (jax.nn.sigmoid(g_a) * (y_a @ w_attn_br[l])
                  + jax.nn.sigmoid(g_h) * (y_h @ w_hy_br[l])
                  + jax.nn.sigmoid(g_c) * (y_c @ w_cf_br[l]))
        mix = merged @ w_o[l] + b_o[l]
        h = layer_norm(ALPHA * h + mix, ln1_g[l], ln1_b[l])
        ffn = grouped_moe(h, w_router, b_router, moe_w_gate[l], moe_w_up[l], moe_w_down[l])
        h = layer_norm(ALPHA * h + ffn, ln2_g[l], ln2_b[l])
    return h
```

```python
import functools
import math

import numpy as np
import jax
import jax.numpy as jnp
from jax import lax
from jax.experimental import pallas as pl
from jax.experimental.pallas import tpu as pltpu

F32 = jnp.float32
BF16 = jnp.bfloat16

GRID_W = 64
NA_HEAD_DIM = 64
NA_KH = 8
NA_KW = 16
HY_ORDER = 2
HY_POS_DIM = 33
HY_FAST_DECAY = 0.3
HY_SLOW_DECAY = 1.5
HY_DECAY_TARGET = 1e-2
CF_K = 31
N_GROUPS = 4
EXPERTS_PER_GROUP = 4
N_EXPERTS = N_GROUPS * EXPERTS_PER_GROUP
LN_EPS = 1e-5
NEG_INF = -1e30

LANES = 128
V7X_VMEM_BYTES = 64 * 1024 * 1024
MIB = 1024 * 1024

ROW_TILE = 256
MM_TM = 1024
MM_TN = 768
DFT_TK = 256
CONV_CB = 256
EXPERT_TM = 256


def _cparams(n_axes, vmem_mib):
    assert vmem_mib * MIB < V7X_VMEM_BYTES
    return pltpu.CompilerParams(dimension_semantics=("arbitrary",) * n_axes,
                                vmem_limit_bytes=vmem_mib * MIB)


def _layer_norm(x, g, b):
    mu = jnp.mean(x, axis=-1, keepdims=True)
    xc = x - mu
    var = jnp.mean(xc * xc, axis=-1, keepdims=True)
    return xc * lax.rsqrt(var + LN_EPS) * g + b


def _sigmoid(x):
    return 1.0 / (1.0 + jnp.exp(-x))


def _in_ln_kernel(x_ref, g_ref, b_ref, h_ref, hb_ref):
    y = _layer_norm(x_ref[...], g_ref[...], b_ref[...])
    h_ref[...] = y
    hb_ref[...] = y.astype(BF16)


def _in_ln(x2, g, b):
    n, d = x2.shape
    row = pl.BlockSpec((ROW_TILE, d), lambda i: (i, 0))
    vec = pl.BlockSpec((1, d), lambda i: (0, 0))
    return pl.pallas_call(
        _in_ln_kernel,
        grid=(n // ROW_TILE,),
        in_specs=[row, vec, vec],
        out_specs=[row, row],
        out_shape=[jax.ShapeDtypeStruct((n, d), F32), jax.ShapeDtypeStruct((n, d), BF16)],
        compiler_params=_cparams(1, 32),
        name="in_ln",
    )(x2, g.reshape(1, d), b.reshape(1, d))


def _mm_bias_kernel(x_ref, w_ref, b_ref, o_ref):
    acc = jnp.dot(x_ref[...], w_ref[...], preferred_element_type=F32)
    o_ref[...] = (acc + b_ref[...]).astype(o_ref.dtype)


def _mm_kernel(x_ref, w_ref, o_ref):
    o_ref[...] = jnp.dot(x_ref[...], w_ref[...], preferred_element_type=F32).astype(o_ref.dtype)


def _matmul(x, w, bias, *, col0, ncols, out_dtype, layer=None, name):
    m, k = x.shape
    tm = min(MM_TM, m)
    tn = MM_TN
    assert m % tm == 0 and ncols % tn == 0 and col0 % tn == 0
    jb = col0 // tn
    x_spec = pl.BlockSpec((tm, k), lambda i, j: (i, 0))
    if layer is None:
        w_spec = pl.BlockSpec((k, tn), lambda i, j: (0, j + jb))
    else:
        w_spec = pl.BlockSpec((None, k, tn), lambda i, j: (layer, 0, j + jb))
    o_spec = pl.BlockSpec((tm, tn), lambda i, j: (i, j))
    args, specs, body = [x, w], [x_spec, w_spec], _mm_kernel
    if bias is not None:
        if layer is None:
            specs.append(pl.BlockSpec((1, tn), lambda i, j: (0, j + jb)))
        else:
            specs.append(pl.BlockSpec((None, 1, tn), lambda i, j: (layer, 0, j + jb)))
        args.append(bias)
        body = _mm_bias_kernel
    return pl.pallas_call(
        body,
        grid=(m // tm, ncols // tn),
        in_specs=specs,
        out_specs=o_spec,
        out_shape=jax.ShapeDtypeStruct((m, ncols), out_dtype),
        compiler_params=_cparams(2, 40),
        name=name,
    )(*args)


def _natten_bias_table(rpb):
    qc = np.arange(GRID_W)
    kc = np.arange(GRID_W)
    cs = np.clip(qc - NA_KW // 2, 0, GRID_W - NA_KW)
    valid = (kc[None, :] >= cs[:, None]) & (kc[None, :] < cs[:, None] + NA_KW)
    col_off = np.clip(kc[None, :] - qc[:, None], -(NA_KW - 1), NA_KW - 1) + (NA_KW - 1)
    row_off = np.arange(NA_KH)[None, :] - np.arange(NA_KH)[:, None] + (NA_KH - 1)
    b = rpb.astype(F32)[:, row_off]
    b = b[..., col_off]
    b = jnp.where(jnp.asarray(valid)[None, None, None], b, NEG_INF)
    h = rpb.shape[0]
    return b.transpose(1, 0, 3, 2, 4).reshape(NA_KH, h, GRID_W, NA_KH * GRID_W)


def _natten_kernel(q_ref, k_ref, v_ref, bias_ref, o_ref, *, rows):
    win = NA_KH * GRID_W
    lane = lax.broadcasted_iota(jnp.int32, (GRID_W, 2 * NA_HEAD_DIM), 1)
    first = lane < NA_HEAD_DIM

    def row_body(r, carry):
        rs = jnp.clip(r - NA_KH // 2, 0, rows - NA_KH)
        d = r - rs
        q = q_ref[0, pl.ds(pl.multiple_of(r * GRID_W, GRID_W), GRID_W), :]
        q = q * jnp.asarray(NA_HEAD_DIM ** -0.5, BF16)
        k0 = pl.multiple_of(rs * GRID_W, GRID_W)
        kw = k_ref[0, pl.ds(k0, win), :]
        vw = v_ref[0, pl.ds(k0, win), :]
        zero = jnp.zeros_like(q)
        outs = []
        for hh in range(2):
            qm = jnp.where(first, q, zero) if hh == 0 else jnp.where(first, zero, q)
            s = lax.dot_general(qm, kw, (((1,), (1,)), ((), ())), preferred_element_type=F32)
            s = s + bias_ref[d, hh]
            m = jnp.max(s, axis=-1, keepdims=True)
            e = jnp.exp(s - m)
            den = jnp.sum(e, axis=-1, keepdims=True)
            o = jnp.dot(e.astype(BF16), vw, preferred_element_type=F32)
            outs.append(o / den)
        o_ref[0, pl.ds(pl.multiple_of(r * GRID_W, GRID_W), GRID_W), :] = (
            jnp.where(first, outs[0], outs[1]).astype(o_ref.dtype))
        return carry

    lax.fori_loop(0, rows, row_body, 0)


def _natten(qkv, bias_tab):
    b, l, w3 = qkv.shape
    w = w3 // 3
    pair = 2 * NA_HEAD_DIM
    npairs = w // pair
    rows = l // GRID_W
    assert rows >= NA_KH
    blk = lambda off: pl.BlockSpec((1, l, pair), lambda bi, hp: (bi, 0, hp + off))
    return pl.pallas_call(
        functools.partial(_natten_kernel, rows=rows),
        grid=(b, npairs),
        in_specs=[blk(0), blk(npairs), blk(2 * npairs),
                  pl.BlockSpec((NA_KH, 2, GRID_W, NA_KH * GRID_W), lambda bi, hp: (0, hp, 0, 0))],
        out_specs=pl.BlockSpec((1, l, pair), lambda bi, hp: (bi, 0, hp)),
        out_shape=jax.ShapeDtypeStruct((b, l, w), BF16),
        compiler_params=_cparams(2, 32),
        name="natten",
    )(qkv, qkv, qkv, bias_tab)


@functools.lru_cache(maxsize=None)
def _dft_matrices(l, tk):
    n2 = 2 * l
    k = np.arange(l, dtype=np.int64)
    n = np.arange(l, dtype=np.int64)
    ang = 2.0 * np.pi * ((k[:, None] * n[None, :]) % n2).astype(np.float64) / n2
    f_re = np.cos(ang)
    f_im = -np.sin(ang)
    f_im[0, :] = np.cos(np.pi * n)
    g_re = (2.0 / n2) * np.cos(ang).T
    g_re[:, 0] = 1.0 / n2
    g_im = -(2.0 / n2) * np.sin(ang).T
    g_im[:, 0] = np.cos(np.pi * n) / n2
    kt = l // tk
    fwd = np.stack([f_re.reshape(kt, tk, l), f_im.reshape(kt, tk, l)], axis=1).reshape(2 * l, l)
    inv = np.stack([g_re.reshape(l, kt, tk), g_im.reshape(l, kt, tk)], axis=2).reshape(l, 2 * l)
    return np.asarray(fwd, dtype=BF16), np.asarray(inv, dtype=BF16)


def _hy_filter_kernel(z_ref, t_ref, dl_ref, w1_ref, b1_ref, w2_ref, b2_ref, w3_ref, b3_ref,
                      fr_ref, w4_ref, o_ref, *, half):
    hp = lax.Precision.HIGHEST
    fr = fr_ref[...]
    h = jnp.sin(fr * (jnp.dot(z_ref[...], w1_ref[...], precision=hp, preferred_element_type=F32) + b1_ref[...]))
    h = jnp.sin(fr * (jnp.dot(h, w2_ref[...], precision=hp, preferred_element_type=F32) + b2_ref[...]))
    h = jnp.sin(fr * (jnp.dot(h, w3_ref[...], precision=hp, preferred_element_type=F32) + b3_ref[...]))
    h = jnp.dot(h, w4_ref[...], precision=hp, preferred_element_type=F32)
    h = h * jnp.exp(-t_ref[...] * dl_ref[...])
    tl, nc = h.shape
    row = lax.broadcasted_iota(jnp.int32, (tl, nc), 0) + pl.program_id(0) * tl
    col = lax.broadcasted_iota(jnp.int32, (tl, nc), 1)
    h = jnp.where((row == 0) & (col >= half), 0.0, h)
    o_ref[...] = h.astype(o_ref.dtype)


def _pad2(a, r, c):
    return jnp.pad(a.astype(F32), ((0, r - a.shape[0]), (0, c - a.shape[1])))


def _hy_filters(l, w1, b1, w2, b2, w3, b3, freq, w4):
    t = jnp.linspace(0.0, 1.0, l, dtype=F32)[:, None]
    bands = (HY_POS_DIM - 1) // 2
    w = 2.0 * math.pi * jnp.arange(l, dtype=F32)[:, None] / l
    f = jnp.linspace(1e-4, bands - 1, bands, dtype=F32)[None, :]
    z = jnp.concatenate([t, jnp.cos(f * w), -jnp.sin(f * w)], axis=-1)
    nc = w4.shape[1]
    max_decay = math.log(HY_DECAY_TARGET) / HY_FAST_DECAY
    min_decay = math.log(HY_DECAY_TARGET) / HY_SLOW_DECAY
    deltas = jnp.abs(jnp.linspace(min_decay, max_decay, nc, dtype=F32))[None, :]
    hid = LANES
    tl = min(256, l)
    full = lambda r, c: pl.BlockSpec((r, c), lambda i: (0, 0))
    return pl.pallas_call(
        functools.partial(_hy_filter_kernel, half=nc // 2),
        grid=(l // tl,),
        in_specs=[pl.BlockSpec((tl, hid), lambda i: (i, 0)), pl.BlockSpec((tl, 1), lambda i: (i, 0)),
                  full(1, nc), full(hid, hid), full(1, hid), full(hid, hid), full(1, hid),
                  full(hid, hid), full(1, hid), full(1, hid), full(hid, nc)],
        out_specs=pl.BlockSpec((tl, nc), lambda i: (i, 0)),
        out_shape=jax.ShapeDtypeStruct((l, nc), BF16),
        compiler_params=_cparams(1, 32),
        name="hy_filter",
    )(_pad2(z, l, hid), t, deltas, _pad2(w1, hid, hid), _pad2(b1[None], 1, hid),
      _pad2(w2, hid, hid), _pad2(b2[None], 1, hid), _pad2(w3, hid, hid), _pad2(b3[None], 1, hid),
      _pad2(freq[None], 1, hid), _pad2(w4, hid, nc))


def _short_conv_kernel(u_ref, w_ref, b_ref, o_ref):
    u = u_ref[0]
    l = u.shape[0]
    row = lax.broadcasted_iota(jnp.int32, u.shape, 0)
    prev = jnp.where(row == 0, 0.0, pltpu.roll(u, 1, 0))
    nxt = jnp.where(row == l - 1, 0.0, pltpu.roll(u, l - 1, 0))
    out = w_ref[0:1, :] * prev + w_ref[1:2, :] * u + w_ref[2:3, :] * nxt + b_ref[...]
    o_ref[0] = out.astype(o_ref.dtype)


def _short_conv(u, w, bias):
    b, l, c = u.shape
    cb = CONV_CB
    return pl.pallas_call(
        _short_conv_kernel,
        grid=(b, c // cb),
        in_specs=[pl.BlockSpec((1, l, cb), lambda bi, ci: (bi, 0, ci)),
                  pl.BlockSpec((3, cb), lambda bi, ci: (0, ci)),
                  pl.BlockSpec((1, cb), lambda bi, ci: (0, ci))],
        out_specs=pl.BlockSpec((1, l, cb), lambda bi, ci: (bi, 0, ci)),
        out_shape=jax.ShapeDtypeStruct((b, l, c), BF16),
        compiler_params=_cparams(2, 32),
        name="hy_short_conv",
    )(u, w.astype(F32), bias.astype(F32).reshape(1, c))


def _long_conv_kernel(x_ref, gate_ref, skip_ref, f_ref, g_ref, hf_ref, hb_ref, o_ref, acc_ref, *, tk):
    kt = pl.program_id(1)

    @pl.when(kt == 0)
    def _():
        acc_ref[...] = jnp.zeros_like(acc_ref)

    xb = x_ref[0]
    z = jnp.dot(f_ref[...], xb, preferred_element_type=F32)
    zr, zi = z[:tk], z[tk:]
    hf = hf_ref[...]
    hb = hb_ref[...]
    hr = hf[:tk] + hb[:tk]
    hi = hf[tk:] - hb[tk:]
    hny = hf[tk:] + hb[tk:]
    row0 = (lax.broadcasted_iota(jnp.int32, zr.shape, 0) == 0) & (kt == 0)
    yr = zr * hr - jnp.where(row0, 0.0, zi * hi)
    yi = jnp.where(row0, zi * hny, zr * hi + zi * hr)
    y = jnp.concatenate([yr, yi], axis=0).astype(BF16)
    acc_ref[...] += jnp.dot(g_ref[...], y, preferred_element_type=F32)

    @pl.when(kt == pl.num_programs(1) - 1)
    def _():
        o_ref[0] = (gate_ref[0].astype(F32) * (acc_ref[...] + xb.astype(F32) * skip_ref[...])).astype(o_ref.dtype)


def _long_conv(x_arr, x_blk, gate_arr, gate_blk, skip, spec, order, fwd, inv):
    b, l, _ = x_arr.shape
    c = skip.shape[-1]
    tk = DFT_TK
    kt = l // tk
    return pl.pallas_call(
        functools.partial(_long_conv_kernel, tk=tk),
        grid=(b, kt),
        in_specs=[pl.BlockSpec((1, l, c), lambda bi, ki: (bi, 0, x_blk)),
                  pl.BlockSpec((1, l, c), lambda bi, ki: (bi, 0, gate_blk)),
                  pl.BlockSpec((1, c), lambda bi, ki: (0, 0)),
                  pl.BlockSpec((2 * tk, l), lambda bi, ki: (ki, 0)),
                  pl.BlockSpec((l, 2 * tk), lambda bi, ki: (0, ki)),
                  pl.BlockSpec((2 * tk, c), lambda bi, ki: (ki, order)),
                  pl.BlockSpec((2 * tk, c), lambda bi, ki: (ki, HY_ORDER + order))],
        out_specs=pl.BlockSpec((1, l, c), lambda bi, ki: (bi, 0, 0)),
        out_shape=jax.ShapeDtypeStruct((b, l, c), BF16),
        scratch_shapes=[pltpu.VMEM((l, c), F32)],
        compiler_params=_cparams(2, 52),
        name=f"hy_long_conv{order}",
    )(x_arr, gate_arr, skip.astype(F32).reshape(1, c), fwd, inv, spec, spec)


def _cf_conv_kernel(a_ref, g_ref, w_ref, b_ref, o_ref, zpad_ref, *, chunk):
    l = a_ref.shape[1]
    pad = 16
    z = a_ref[0] * _sigmoid(g_ref[0])
    zeros = jnp.zeros((pad, z.shape[1]), F32)
    zpad_ref[0:pad, :] = zeros
    zpad_ref[pad:pad + l, :] = z
    zpad_ref[pad + l:pad + l + pad, :] = zeros
    half = CF_K // 2
    for c in range(l // chunk):
        acc = jnp.broadcast_to(b_ref[...], (chunk, z.shape[1]))
        for k in range(CF_K):
            s = c * chunk + pad + k - half
            acc = acc + w_ref[k:k + 1, :] * zpad_ref[s:s + chunk, :]
        o_ref[0, c * chunk:(c + 1) * chunk, :] = acc


def _cf_conv(cf_in, w, bias):
    b, l, w2 = cf_in.shape
    wd = w2 // 2
    cb = CONV_CB
    nb = wd // cb
    chunk = min(256, l)
    return pl.pallas_call(
        functools.partial(_cf_conv_kernel, chunk=chunk),
        grid=(b, nb),
        in_specs=[pl.BlockSpec((1, l, cb), lambda bi, ci: (bi, 0, ci)),
                  pl.BlockSpec((1, l, cb), lambda bi, ci: (bi, 0, ci + nb)),
                  pl.BlockSpec((CF_K, cb), lambda bi, ci: (0, ci)),
                  pl.BlockSpec((1, cb), lambda bi, ci: (0, ci))],
        out_specs=pl.BlockSpec((1, l, cb), lambda bi, ci: (bi, 0, ci)),
        out_shape=jax.ShapeDtypeStruct((b, l, wd), F32),
        scratch_shapes=[pltpu.VMEM((l + 32, cb), F32)],
        compiler_params=_cparams(2, 32),
        name="cf_conv",
    )(cf_in, cf_in, w.astype(F32), bias.astype(F32).reshape(1, wd))


def _merge_kernel(ya_ref, yh_ref, yc_ref, ga_ref, gh_ref, gc_ref, h_ref, wa_ref, wh_ref, wc_ref,
                  wo_ref, bo_ref, cg_ref, cb_ref, g1_ref, b1_ref, o_ref, *, alpha):
    yc = _layer_norm(yc_ref[...], cg_ref[...], cb_ref[...])
    yc = (yc * _sigmoid(yc)).astype(BF16)
    m = _sigmoid(ga_ref[...].astype(F32)) * jnp.dot(ya_ref[...], wa_ref[...], preferred_element_type=F32)
    m = m + _sigmoid(gh_ref[...].astype(F32)) * jnp.dot(yh_ref[...], wh_ref[...], preferred_element_type=F32)
    m = m + _sigmoid(gc_ref[...].astype(F32)) * jnp.dot(yc, wc_ref[...], preferred_element_type=F32)
    mix = jnp.dot(m.astype(BF16), wo_ref[...], preferred_element_type=F32) + bo_ref[...]
    o_ref[...] = _layer_norm(alpha * h_ref[...] + mix, g1_ref[...], b1_ref[...])


def _merge(ya, yh, yc, gates, h, wa, wh, wc, wo, bo, cg, cb, g1, b1, layer, alpha):
    n, d = h.shape
    w = ya.shape[1]
    tm = ROW_TILE
    row = lambda c: pl.BlockSpec((tm, c), lambda i: (i, 0))
    gate = lambda j: pl.BlockSpec((tm, d), lambda i: (i, j))
    once = pl.Buffered(1)
    wspec = lambda r, c: pl.BlockSpec((None, r, c), lambda i: (layer, 0, 0), pipeline_mode=once)
    vec = lambda c: pl.BlockSpec((None, 1, c), lambda i: (layer, 0, 0))
    return pl.pallas_call(
        functools.partial(_merge_kernel, alpha=alpha),
        grid=(n // tm,),
        in_specs=[row(w), row(w), row(w), gate(0), gate(1), gate(2), row(d),
                  wspec(w, d), wspec(w, d), wspec(w, d), wspec(d, d),
                  vec(d), vec(w), vec(w), vec(d), vec(d)],
        out_specs=row(d),
        out_shape=jax.ShapeDtypeStruct((n, d), F32),
        compiler_params=_cparams(1, 48),
        name="merge",
    )(ya, yh, yc, gates, gates, gates, h, wa, wh, wc, wo, bo, cg, cb, g1, b1)


def _first_max(vals):
    m = vals[0]
    for v in vals[1:]:
        m = jnp.maximum(m, v)
    idx = jnp.full(m.shape, len(vals) - 1, jnp.int32)
    for j in range(len(vals) - 2, -1, -1):
        idx = jnp.where(vals[j] == m, j, idx)
    return m, idx


def _top2(vals):
    m1, i1 = _first_max(vals)
    rest = [jnp.where(i1 == j, -1.0, v) for j, v in enumerate(vals)]
    m2, i2 = _first_max(rest)
    return m1, i1, m2, i2


def _router_kernel(h_ref, wr_ref, br_ref, idx_ref, wgt_ref, cnt_ref, carry_ref):
    i = pl.program_id(0)

    @pl.when(i == 0)
    def _():
        carry_ref[...] = jnp.zeros_like(carry_ref)

    logits = jnp.dot(h_ref[...], wr_ref[...], precision=lax.Precision.HIGHEST,
                     preferred_element_type=F32) + br_ref[...]
    lt = logits.T[:N_EXPERTS]
    tm = lt.shape[1]
    mx = jnp.max(lt, axis=0, keepdims=True)
    ex = jnp.exp(lt - mx)
    probs = ex / jnp.sum(ex, axis=0, keepdims=True)
    p = [probs[e:e + 1, :] for e in range(N_EXPERTS)]
    scores = []
    for g in range(N_GROUPS):
        a, _, b, _ = _top2(p[g * EXPERTS_PER_GROUP:(g + 1) * EXPERTS_PER_GROUP])
        scores.append(a + b)
    _, g_sel = _first_max(scores)
    pg = []
    for j in range(EXPERTS_PER_GROUP):
        v = p[(N_GROUPS - 1) * EXPERTS_PER_GROUP + j]
        for g in range(N_GROUPS - 2, -1, -1):
            v = jnp.where(g_sel == g, p[g * EXPERTS_PER_GROUP + j], v)
        pg.append(v)
    p1, i1, p2, i2 = _top2(pg)
    den = p1 + p2
    e0 = g_sel * EXPERTS_PER_GROUP + i1
    e1 = g_sel * EXPERTS_PER_GROUP + i2

    erow = lax.broadcasted_iota(jnp.int32, (N_EXPERTS, tm), 0)
    oh0 = (erow == e0).astype(F32)
    oh1 = (erow == e1).astype(F32)
    both = oh0 + oh1
    before = (lax.broadcasted_iota(jnp.int32, (tm, tm), 0) < lax.broadcasted_iota(jnp.int32, (tm, tm), 1))
    cum = jnp.dot(both.astype(BF16), before.astype(BF16), preferred_element_type=F32) + carry_ref[:, 0:1]
    r0 = jnp.sum(oh0 * cum, axis=0, keepdims=True)
    r1 = jnp.sum(oh1 * cum, axis=0, keepdims=True)
    carry_ref[...] = carry_ref[...] + jnp.sum(both, axis=1, keepdims=True)
    cnt_ref[...] = carry_ref[...]

    zi = jnp.zeros((4, tm), jnp.int32)
    idx_ref[...] = jnp.concatenate([e0, e1, r0.astype(jnp.int32), r1.astype(jnp.int32), zi], axis=0)
    zf = jnp.zeros((6, tm), F32)
    wgt_ref[...] = jnp.concatenate([p1 / den, p2 / den, zf], axis=0)


def _router(h, w_router, b_router):
    n, d = h.shape
    tm = ROW_TILE
    wr = _pad2(w_router, d, LANES)
    br = _pad2(b_router[None], 1, LANES)
    return pl.pallas_call(
        _router_kernel,
        grid=(n // tm,),
        in_specs=[pl.BlockSpec((tm, d), lambda i: (i, 0)),
                  pl.BlockSpec((d, LANES), lambda i: (0, 0)),
                  pl.BlockSpec((1, LANES), lambda i: (0, 0))],
        out_specs=[pl.BlockSpec((8, tm), lambda i: (0, i)),
                   pl.BlockSpec((8, tm), lambda i: (0, i)),
                   pl.BlockSpec((N_EXPERTS, LANES), lambda i: (0, 0))],
        out_shape=[jax.ShapeDtypeStruct((8, n), jnp.int32),
                   jax.ShapeDtypeStruct((8, n), F32),
                   jax.ShapeDtypeStruct((N_EXPERTS, LANES), F32)],
        scratch_shapes=[pltpu.VMEM((N_EXPERTS, LANES), F32)],
        compiler_params=_cparams(1, 32),
        name="router",
    )(h, wr, br)


def _expert_kernel(texp_ref, nv_ref, stok_ref, h_hbm, wg_ref, wu_ref, wd_ref, o_ref, xbuf, sem, *, tm):
    i = pl.program_id(0)
    nv = nv_ref[0]

    def issue(tile, slot):
        def body(r, carry):
            tok = stok_ref[tile * tm + r]
            pltpu.make_async_copy(h_hbm.at[pl.ds(tok, 1), :], xbuf.at[slot, pl.ds(r, 1), :],
                                  sem.at[slot]).start()
            return carry
        lax.fori_loop(0, tm, body, 0)

    @pl.when(i == 0)
    def _():
        issue(0, 0)

    @pl.when(i + 1 < nv)
    def _():
        issue(i + 1, (i + 1) % 2)

    @pl.when(i < nv)
    def _():
        slot = i % 2
        pltpu.make_async_copy(h_hbm.at[pl.ds(0, tm), :], xbuf.at[slot], sem.at[slot]).wait()
        x = xbuf[slot].astype(BF16)
        g = jnp.dot(x, wg_ref[...], preferred_element_type=F32)
        u = jnp.dot(x, wu_ref[...], preferred_element_type=F32)
        hid = (g * _sigmoid(g) * u).astype(BF16)
        o_ref[...] = jnp.dot(hid, wd_ref[...], preferred_element_type=F32)

    @pl.when(i >= nv)
    def _():
        o_ref[...] = jnp.zeros_like(o_ref)


def _experts(h, texp, nvalid, slot_tok, wg, wu, wd, layer):
    n, d = h.shape
    de = wg.shape[-1]
    tm = EXPERT_TM
    n_tiles = slot_tok.shape[0] // tm
    grid_spec = pltpu.PrefetchScalarGridSpec(
        num_scalar_prefetch=3,
        grid=(n_tiles,),
        in_specs=[pl.BlockSpec(memory_space=pl.ANY),
                  pl.BlockSpec((None, None, d, de), lambda i, te, nv, st: (layer, te[i], 0, 0)),
                  pl.BlockSpec((None, None, d, de), lambda i, te, nv, st: (layer, te[i], 0, 0)),
                  pl.BlockSpec((None, None, de, d), lambda i, te, nv, st: (layer, te[i], 0, 0))],
        out_specs=pl.BlockSpec((tm, d), lambda i, te, nv, st: (i, 0)),
        scratch_shapes=[pltpu.VMEM((2, tm, d), F32), pltpu.SemaphoreType.DMA((2,))],
    )
    return pl.pallas_call(
        functools.partial(_expert_kernel, tm=tm),
        grid_spec=grid_spec,
        out_shape=jax.ShapeDtypeStruct((n_tiles * tm, d), F32),
        compiler_params=_cparams(1, 48),
        name="experts",
    )(texp, nvalid, slot_tok, h, wg, wu, wd)


def _combine_kernel(p0_ref, p1_ref, ys_hbm, h_ref, w0_ref, w1_ref, g_ref, b_ref, o_ref, ob_ref,
                    gbuf, sem, *, tm, alpha):
    i = pl.program_id(0)
    nt = pl.num_programs(0)

    def issue(tile, slot):
        def body(r, carry):
            t = tile * tm + r
            pltpu.make_async_copy(ys_hbm.at[pl.ds(p0_ref[t], 1), :], gbuf.at[slot, 0, pl.ds(r, 1), :],
                                  sem.at[slot]).start()
            pltpu.make_async_copy(ys_hbm.at[pl.ds(p1_ref[t], 1), :], gbuf.at[slot, 1, pl.ds(r, 1), :],
                                  sem.at[slot]).start()
            return carry
        lax.fori_loop(0, tm, body, 0)

    @pl.when(i == 0)
    def _():
        issue(0, 0)

    @pl.when(i + 1 < nt)
    def _():
        issue(i + 1, (i + 1) % 2)

    slot = i % 2
    for j in range(2):
        pltpu.make_async_copy(ys_hbm.at[pl.ds(0, tm), :], gbuf.at[slot, j], sem.at[slot]).wait()
    y = w0_ref[...] * gbuf[slot, 0] + w1_ref[...] * gbuf[slot, 1]
    out = _layer_norm(alpha * h_ref[...] + y, g_ref[...], b_ref[...])
    o_ref[...] = out
    ob_ref[...] = out.astype(BF16)


def _combine(ys, h, pos0, pos1, w0, w1, g2, b2, layer, alpha):
    n, d = h.shape
    tm = ROW_TILE
    row = pl.BlockSpec((tm, d), lambda i, a, b: (i, 0))
    col = pl.BlockSpec((tm, 1), lambda i, a, b: (i, 0))
    vec = pl.BlockSpec((None, 1, d), lambda i, a, b: (layer, 0, 0))
    grid_spec = pltpu.PrefetchScalarGridSpec(
        num_scalar_prefetch=2,
        grid=(n // tm,),
        in_specs=[pl.BlockSpec(memory_space=pl.ANY), row, col, col, vec, vec],
        out_specs=[row, row],
        scratch_shapes=[pltpu.VMEM((2, 2, tm, d), F32), pltpu.SemaphoreType.DMA((2,))],
    )
    return pl.pallas_call(
        functools.partial(_combine_kernel, tm=tm, alpha=alpha),
        grid_spec=grid_spec,
        out_shape=[jax.ShapeDtypeStruct((n, d), F32), jax.ShapeDtypeStruct((n, d), BF16)],
        compiler_params=_cparams(1, 40),
        name="combine",
    )(pos0, pos1, ys, h, w0, w1, g2, b2)


def _routing_tables(idx, cnt, n):
    tm = EXPERT_TM
    e0, e1, r0, r1 = idx[0], idx[1], idx[2], idx[3]
    counts = cnt[:, 0].astype(jnp.int32)
    padded = ((counts + tm - 1) // tm) * tm
    pend = jnp.cumsum(padded)
    poff = pend - padded
    pos0 = poff[e0] + r0
    pos1 = poff[e1] + r1
    n_slots = 2 * n + N_EXPERTS * tm
    tok = jnp.arange(n, dtype=jnp.int32)
    slot_tok = jnp.zeros((n_slots,), jnp.int32).at[pos0].set(tok).at[pos1].set(tok)
    n_tiles = n_slots // tm
    nvalid = pend[-1] // tm
    tile = jnp.arange(n_tiles, dtype=jnp.int32)
    texp = jnp.searchsorted(pend, jnp.minimum(tile, nvalid - 1) * tm, side="right").astype(jnp.int32)
    texp = jnp.minimum(texp, N_EXPERTS - 1)
    return pos0, pos1, slot_tok, texp, nvalid.reshape(1).astype(jnp.int32)


def kernel(x, in_ln_g, in_ln_b, w_in, b_in, attn_rpb, hy_conv_w, hy_conv_b, hy_f_w1, hy_f_b1, hy_f_w2, hy_f_b2, hy_f_w3, hy_f_b3, hy_f_freq, hy_f_w4, hy_skip, cf_dw_w, cf_dw_b, cf_ln_g, cf_ln_b, w_attn_br, w_hy_br, w_cf_br, w_o, b_o, ln1_g, ln1_b, w_router, b_router, moe_w_gate, moe_w_up, moe_w_down, ln2_g, ln2_b):
    bsz, l, d = x.shape
    depth = w_in.shape[0]
    n = bsz * l
    mw = w_attn_br.shape[1]
    alpha = (2 * depth) ** 0.25
    c_qkv, c_hy, c_cf, c_gate = 0, 3 * mw, 6 * mw, 8 * mw

    w_in_b = w_in.astype(BF16)
    b_in3 = b_in.astype(F32)[:, None, :]
    wa_b, wh_b, wc_b, wo_b = (w.astype(BF16) for w in (w_attn_br, w_hy_br, w_cf_br, w_o))
    wg_b, wu_b, wd_b = (w.astype(BF16) for w in (moe_w_gate, moe_w_up, moe_w_down))
    vec3 = lambda v: v.astype(F32)[:, None, :]
    bo3, cg3, cb3, g13, b13, g23, b23 = map(vec3, (b_o, cf_ln_g, cf_ln_b, ln1_g, ln1_b, ln2_g, ln2_b))
    fwd_np, inv_np = _dft_matrices(l, DFT_TK)
    fwd, inv = jnp.asarray(fwd_np), jnp.asarray(inv_np)

    h, hb = _in_ln(x.reshape(n, d), in_ln_g, in_ln_b)
    for layer in range(depth):
        proj = lambda c0, nc, dt, nm: _matmul(hb, w_in_b, b_in3, col0=c0, ncols=nc, out_dtype=dt,
                                              layer=layer, name=nm)
        qkv = proj(c_qkv, 3 * mw, BF16, "proj_qkv")
        hy_in = proj(c_hy, 3 * mw, F32, "proj_hyena")
        cf_in = proj(c_cf, 2 * mw, F32, "proj_conformer")
        gates = proj(c_gate, 3 * d, BF16, "proj_gates")

        y_a = _natten(qkv.reshape(bsz, l, 3 * mw), _natten_bias_table(attn_rpb[layer]))

        taps = _hy_filters(l, hy_f_w1[layer], hy_f_b1[layer], hy_f_w2[layer], hy_f_b2[layer],
                           hy_f_w3[layer], hy_f_b3[layer], hy_f_freq[layer], hy_f_w4[layer])
        spec = _matmul(fwd, taps, None, col0=0, ncols=taps.shape[1], out_dtype=F32, name="hy_spectrum")
        uc = _short_conv(hy_in.reshape(bsz, l, 3 * mw), hy_conv_w[layer], hy_conv_b[layer])
        z1 = _long_conv(uc, 0, uc, 1, hy_skip[layer, 0], spec, 0, fwd, inv)
        y_h = _long_conv(z1, 0, uc, 2, hy_skip[layer, 1], spec, 1, fwd, inv)

        y_c = _cf_conv(cf_in.reshape(bsz, l, 2 * mw), cf_dw_w[layer], cf_dw_b[layer])

        h = _merge(y_a.reshape(n, mw), y_h.reshape(n, mw), y_c.reshape(n, mw), gates, h,
                   wa_b, wh_b, wc_b, wo_b, bo3, cg3, cb3, g13, b13, layer, alpha)

        idx, wgt, cnt = _router(h, w_router, b_router)
        pos0, pos1, slot_tok, texp, nvalid = _routing_tables(idx, cnt, n)
        ys = _experts(h, texp, nvalid, slot_tok, wg_b, wu_b, wd_b, layer)
        h, hb = _combine(ys, h, pos0, pos1, wgt[0].reshape(n, 1), wgt[1].reshape(n, 1),
                         g23, b23, layer, alpha)
    return h.reshape(bsz, l, d)
```

```python
import functools
import math

import numpy as np
import jax
import jax.numpy as jnp
from jax import lax
from jax.experimental import pallas as pl
from jax.experimental.pallas import tpu as pltpu

F32 = jnp.float32
BF16 = jnp.bfloat16

GRID_W = 64
NA_HEAD_DIM = 64
NA_KH = 8
NA_KW = 16
HY_ORDER = 2
HY_POS_DIM = 33
HY_FAST_DECAY = 0.3
HY_SLOW_DECAY = 1.5
HY_DECAY_TARGET = 1e-2
CF_K = 31
N_GROUPS = 4
EXPERTS_PER_GROUP = 4
N_EXPERTS = N_GROUPS * EXPERTS_PER_GROUP
LN_EPS = 1e-5
NEG_INF = -1e30

LANES = 128
V7X_VMEM_BYTES = 64 * 1024 * 1024
MIB = 1024 * 1024

ROW_TILE = 256
MM_TM = 1024
MM_TN = 768
DFT_TK = 256
CONV_CB = 256
EXPERT_TM = 512
GATHER_UNROLL = 8
NATTEN_UNROLL = 8


def _cparams(n_axes, vmem_mib, row_gather=False):
    assert vmem_mib * MIB < V7X_VMEM_BYTES
    return pltpu.CompilerParams(dimension_semantics=("arbitrary",) * n_axes,
                                vmem_limit_bytes=vmem_mib * MIB,
                                disable_bounds_checks=row_gather)


def _layer_norm(x, g, b):
    mu = jnp.mean(x, axis=-1, keepdims=True)
    xc = x - mu
    var = jnp.mean(xc * xc, axis=-1, keepdims=True)
    return xc * lax.rsqrt(var + LN_EPS) * g + b


def _sigmoid(x):
    return 1.0 / (1.0 + jnp.exp(-x))


def _in_ln_kernel(x_ref, g_ref, b_ref, h_ref, hb_ref):
    y = _layer_norm(x_ref[...], g_ref[...], b_ref[...])
    h_ref[...] = y
    hb_ref[...] = y.astype(BF16)


def _in_ln(x2, g, b):
    n, d = x2.shape
    row = pl.BlockSpec((ROW_TILE, d), lambda i: (i, 0))
    vec = pl.BlockSpec((1, d), lambda i: (0, 0))
    return pl.pallas_call(
        _in_ln_kernel,
        grid=(n // ROW_TILE,),
        in_specs=[row, vec, vec],
        out_specs=[row, row],
        out_shape=[jax.ShapeDtypeStruct((n, d), F32), jax.ShapeDtypeStruct((n, d), BF16)],
        compiler_params=_cparams(1, 32),
        name="in_ln",
    )(x2, g.reshape(1, d), b.reshape(1, d))


def _mm_bias_kernel(x_ref, w_ref, b_ref, o_ref):
    acc = jnp.dot(x_ref[...], w_ref[...], preferred_element_type=F32)
    o_ref[...] = (acc + b_ref[...]).astype(o_ref.dtype)


def _mm_kernel(x_ref, w_ref, o_ref):
    o_ref[...] = jnp.dot(x_ref[...], w_ref[...], preferred_element_type=F32).astype(o_ref.dtype)


def _matmul(x, w, bias, *, col0, ncols, out_dtype, layer=None, name):
    m, k = x.shape
    tm = min(MM_TM, m)
    tn = MM_TN
    assert m % tm == 0 and ncols % tn == 0 and col0 % tn == 0
    jb = col0 // tn
    x_spec = pl.BlockSpec((tm, k), lambda i, j: (i, 0))
    if layer is None:
        w_spec = pl.BlockSpec((k, tn), lambda i, j: (0, j + jb))
    else:
        w_spec = pl.BlockSpec((None, k, tn), lambda i, j: (layer, 0, j + jb))
    o_spec = pl.BlockSpec((tm, tn), lambda i, j: (i, j))
    args, specs, body = [x, w], [x_spec, w_spec], _mm_kernel
    if bias is not None:
        if layer is None:
            specs.append(pl.BlockSpec((1, tn), lambda i, j: (0, j + jb)))
        else:
            specs.append(pl.BlockSpec((None, 1, tn), lambda i, j: (layer, 0, j + jb)))
        args.append(bias)
        body = _mm_bias_kernel
    return pl.pallas_call(
        body,
        grid=(m // tm, ncols // tn),
        in_specs=specs,
        out_specs=o_spec,
        out_shape=jax.ShapeDtypeStruct((m, ncols), out_dtype),
        compiler_params=_cparams(2, 40),
        name=name,
    )(*args)


def _natten_bias_table(rpb):
    qc = np.arange(GRID_W)
    kc = np.arange(GRID_W)
    cs = np.clip(qc - NA_KW // 2, 0, GRID_W - NA_KW)
    valid = (kc[None, :] >= cs[:, None]) & (kc[None, :] < cs[:, None] + NA_KW)
    col_off = np.clip(kc[None, :] - qc[:, None], -(NA_KW - 1), NA_KW - 1) + (NA_KW - 1)
    row_off = np.arange(NA_KH)[None, :] - np.arange(NA_KH)[:, None] + (NA_KH - 1)
    b = rpb.astype(F32)[:, row_off]
    b = b[..., col_off]
    b = jnp.where(jnp.asarray(valid)[None, None, None], b, NEG_INF)
    h = rpb.shape[0]
    return b.transpose(1, 0, 3, 2, 4).reshape(NA_KH, h, GRID_W, NA_KH * GRID_W)


def _natten_kernel(q_ref, k_ref, v_ref, bias_ref, o_ref, *, rows):
    win = NA_KH * GRID_W
    lane = lax.broadcasted_iota(jnp.int32, (GRID_W, 2 * NA_HEAD_DIM), 1)
    first = lane < NA_HEAD_DIM

    def row_body(r, carry):
        rs = jnp.clip(r - NA_KH // 2, 0, rows - NA_KH)
        d = r - rs
        q = q_ref[0, pl.ds(pl.multiple_of(r * GRID_W, GRID_W), GRID_W), :]
        q = q * jnp.asarray(NA_HEAD_DIM ** -0.5, BF16)
        k0 = pl.multiple_of(rs * GRID_W, GRID_W)
        kw = k_ref[0, pl.ds(k0, win), :]
        vw = v_ref[0, pl.ds(k0, win), :]
        zero = jnp.zeros_like(q)
        qm = jnp.concatenate([jnp.where(first, q, zero), jnp.where(first, zero, q)], axis=0)
        s = lax.dot_general(qm, kw, (((1,), (1,)), ((), ())), preferred_element_type=F32)
        s = s + bias_ref[d].reshape(2 * GRID_W, win)
        m = jnp.max(s, axis=-1, keepdims=True)
        e = jnp.exp(s - m)
        den = jnp.sum(e, axis=-1, keepdims=True)
        o = jnp.dot(e.astype(BF16), vw, preferred_element_type=F32) / den
        o_ref[0, pl.ds(pl.multiple_of(r * GRID_W, GRID_W), GRID_W), :] = (
            jnp.where(first, o[:GRID_W], o[GRID_W:]).astype(o_ref.dtype))
        return carry

    lax.fori_loop(0, rows, row_body, 0, unroll=NATTEN_UNROLL)


def _natten(qkv, bias_tab):
    b, l, w3 = qkv.shape
    w = w3 // 3
    pair = 2 * NA_HEAD_DIM
    npairs = w // pair
    rows = l // GRID_W
    assert rows >= NA_KH
    blk = lambda off: pl.BlockSpec((1, l, pair), lambda bi, hp: (bi, 0, hp + off))
    return pl.pallas_call(
        functools.partial(_natten_kernel, rows=rows),
        grid=(b, npairs),
        in_specs=[blk(0), blk(npairs), blk(2 * npairs),
                  pl.BlockSpec((NA_KH, 2, GRID_W, NA_KH * GRID_W), lambda bi, hp: (0, hp, 0, 0))],
        out_specs=pl.BlockSpec((1, l, pair), lambda bi, hp: (bi, 0, hp)),
        out_shape=jax.ShapeDtypeStruct((b, l, w), BF16),
        compiler_params=_cparams(2, 32),
        name="natten",
    )(qkv, qkv, qkv, bias_tab)


@functools.lru_cache(maxsize=None)
def _dft_matrices(l, tk):
    n2 = 2 * l
    k = np.arange(l, dtype=np.int64)
    n = np.arange(l, dtype=np.int64)
    ang = 2.0 * np.pi * ((k[:, None] * n[None, :]) % n2).astype(np.float64) / n2
    f_re = np.cos(ang)
    f_im = -np.sin(ang)
    f_im[0, :] = np.cos(np.pi * n)
    g_re = (2.0 / n2) * np.cos(ang).T
    g_re[:, 0] = 1.0 / n2
    g_im = -(2.0 / n2) * np.sin(ang).T
    g_im[:, 0] = np.cos(np.pi * n) / n2
    kt = l // tk
    fwd = np.stack([f_re.reshape(kt, tk, l), f_im.reshape(kt, tk, l)], axis=1).reshape(2 * l, l)
    inv = np.stack([g_re.reshape(l, kt, tk), g_im.reshape(l, kt, tk)], axis=2).reshape(l, 2 * l)
    return np.asarray(fwd, dtype=BF16), np.asarray(inv, dtype=BF16)


def _hy_filter_kernel(z_ref, t_ref, dl_ref, w1_ref, b1_ref, w2_ref, b2_ref, w3_ref, b3_ref,
                      fr_ref, w4_ref, o_ref, *, half):
    hp = lax.Precision.HIGHEST
    fr = fr_ref[...]
    h = jnp.sin(fr * (jnp.dot(z_ref[...], w1_ref[...], precision=hp, preferred_element_type=F32) + b1_ref[...]))
    h = jnp.sin(fr * (jnp.dot(h, w2_ref[...], precision=hp, preferred_element_type=F32) + b2_ref[...]))
    h = jnp.sin(fr * (jnp.dot(h, w3_ref[...], precision=hp, preferred_element_type=F32) + b3_ref[...]))
    h = jnp.dot(h, w4_ref[...], precision=hp, preferred_element_type=F32)
    h = h * jnp.exp(-t_ref[...] * dl_ref[...])
    tl, nc = h.shape
    row = lax.broadcasted_iota(jnp.int32, (tl, nc), 0) + pl.program_id(0) * tl
    col = lax.broadcasted_iota(jnp.int32, (tl, nc), 1)
    h = jnp.where((row == 0) & (col >= half), 0.0, h)
    o_ref[...] = h.astype(o_ref.dtype)


def _pad2(a, r, c):
    return jnp.pad(a.astype(F32), ((0, r - a.shape[0]), (0, c - a.shape[1])))


def _hy_filters(l, w1, b1, w2, b2, w3, b3, freq, w4):
    t = jnp.linspace(0.0, 1.0, l, dtype=F32)[:, None]
    bands = (HY_POS_DIM - 1) // 2
    w = 2.0 * math.pi * jnp.arange(l, dtype=F32)[:, None] / l
    f = jnp.linspace(1e-4, bands - 1, bands, dtype=F32)[None, :]
    z = jnp.concatenate([t, jnp.cos(f * w), -jnp.sin(f * w)], axis=-1)
    nc = w4.shape[1]
    max_decay = math.log(HY_DECAY_TARGET) / HY_FAST_DECAY
    min_decay = math.log(HY_DECAY_TARGET) / HY_SLOW_DECAY
    deltas = jnp.abs(jnp.linspace(min_decay, max_decay, nc, dtype=F32))[None, :]
    hid = LANES
    tl = min(256, l)
    full = lambda r, c: pl.BlockSpec((r, c), lambda i: (0, 0))
    return pl.pallas_call(
        functools.partial(_hy_filter_kernel, half=nc // 2),
        grid=(l // tl,),
        in_specs=[pl.BlockSpec((tl, hid), lambda i: (i, 0)), pl.BlockSpec((tl, 1), lambda i: (i, 0)),
                  full(1, nc), full(hid, hid), full(1, hid), full(hid, hid), full(1, hid),
                  full(hid, hid), full(1, hid), full(1, hid), full(hid, nc)],
        out_specs=pl.BlockSpec((tl, nc), lambda i: (i, 0)),
        out_shape=jax.ShapeDtypeStruct((l, nc), BF16),
        compiler_params=_cparams(1, 32),
        name="hy_filter",
    )(_pad2(z, l, hid), t, deltas, _pad2(w1, hid, hid), _pad2(b1[None], 1, hid),
      _pad2(w2, hid, hid), _pad2(b2[None], 1, hid), _pad2(w3, hid, hid), _pad2(b3[None], 1, hid),
      _pad2(freq[None], 1, hid), _pad2(w4, hid, nc))


def _short_conv_kernel(u_ref, w_ref, b_ref, o_ref):
    u = u_ref[0]
    l = u.shape[0]
    row = lax.broadcasted_iota(jnp.int32, u.shape, 0)
    prev = jnp.where(row == 0, 0.0, pltpu.roll(u, 1, 0))
    nxt = jnp.where(row == l - 1, 0.0, pltpu.roll(u, l - 1, 0))
    out = w_ref[0:1, :] * prev + w_ref[1:2, :] * u + w_ref[2:3, :] * nxt + b_ref[...]
    o_ref[0] = out.astype(o_ref.dtype)


def _short_conv(u, w, bias):
    b, l, c = u.shape
    cb = CONV_CB
    return pl.pallas_call(
        _short_conv_kernel,
        grid=(b, c // cb),
        in_specs=[pl.BlockSpec((1, l, cb), lambda bi, ci: (bi, 0, ci)),
                  pl.BlockSpec((3, cb), lambda bi, ci: (0, ci)),
                  pl.BlockSpec((1, cb), lambda bi, ci: (0, ci))],
        out_specs=pl.BlockSpec((1, l, cb), lambda bi, ci: (bi, 0, ci)),
        out_shape=jax.ShapeDtypeStruct((b, l, c), BF16),
        compiler_params=_cparams(2, 32),
        name="hy_short_conv",
    )(u, w.astype(F32), bias.astype(F32).reshape(1, c))


def _long_conv_kernel(x_ref, gate_ref, skip_ref, f_ref, g_ref, hf_ref, hb_ref, o_ref, acc_ref, *, tk):
    kt = pl.program_id(1)

    @pl.when(kt == 0)
    def _():
        acc_ref[...] = jnp.zeros_like(acc_ref)

    xb = x_ref[0]
    z = jnp.dot(f_ref[...], xb, preferred_element_type=F32)
    zr, zi = z[:tk], z[tk:]
    hf = hf_ref[...]
    hb = hb_ref[...]
    hr = hf[:tk] + hb[:tk]
    hi = hf[tk:] - hb[tk:]
    hny = hf[tk:] + hb[tk:]
    row0 = (lax.broadcasted_iota(jnp.int32, zr.shape, 0) == 0) & (kt == 0)
    yr = zr * hr - jnp.where(row0, 0.0, zi * hi)
    yi = jnp.where(row0, zi * hny, zr * hi + zi * hr)
    y = jnp.concatenate([yr, yi], axis=0).astype(BF16)
    acc_ref[...] += jnp.dot(g_ref[...], y, preferred_element_type=F32)

    @pl.when(kt == pl.num_programs(1) - 1)
    def _():
        o_ref[0] = (gate_ref[0].astype(F32) * (acc_ref[...] + xb.astype(F32) * skip_ref[...])).astype(o_ref.dtype)


def _long_conv(x_arr, x_blk, gate_arr, gate_blk, skip, spec, order, fwd, inv):
    b, l, _ = x_arr.shape
    c = skip.shape[-1]
    tk = DFT_TK
    kt = l // tk
    return pl.pallas_call(
        functools.partial(_long_conv_kernel, tk=tk),
        grid=(b, kt),
        in_specs=[pl.BlockSpec((1, l, c), lambda bi, ki: (bi, 0, x_blk)),
                  pl.BlockSpec((1, l, c), lambda bi, ki: (bi, 0, gate_blk)),
                  pl.BlockSpec((1, c), lambda bi, ki: (0, 0)),
                  pl.BlockSpec((2 * tk, l), lambda bi, ki: (ki, 0)),
                  pl.BlockSpec((l, 2 * tk), lambda bi, ki: (0, ki)),
                  pl.BlockSpec((2 * tk, c), lambda bi, ki: (ki, order)),
                  pl.BlockSpec((2 * tk, c), lambda bi, ki: (ki, HY_ORDER + order))],
        out_specs=pl.BlockSpec((1, l, c), lambda bi, ki: (bi, 0, 0)),
        out_shape=jax.ShapeDtypeStruct((b, l, c), BF16),
        scratch_shapes=[pltpu.VMEM((l, c), F32)],
        compiler_params=_cparams(2, 52),
        name=f"hy_long_conv{order}",
    )(x_arr, gate_arr, skip.astype(F32).reshape(1, c), fwd, inv, spec, spec)


def _cf_conv_kernel(a_ref, g_ref, w_ref, b_ref, o_ref, zpad_ref, zsh_ref, *, chunk):
    l = a_ref.shape[1]
    pad = 16
    sub = 8
    z = a_ref[0] * _sigmoid(g_ref[0])
    zeros = jnp.zeros((pad, z.shape[1]), F32)
    zpad_ref[0:pad, :] = zeros
    zpad_ref[pad:pad + l, :] = z
    zpad_ref[pad + l:pad + l + pad, :] = zeros
    half = CF_K // 2
    nsh = zsh_ref.shape[0]
    for s in range(sub):
        zsh_ref[...] = zpad_ref[s:s + nsh, :]
        taps = [k for k in range(CF_K) if (pad - half + k) % sub == s]
        for c in range(l // chunk):
            if s == 0:
                acc = jnp.broadcast_to(b_ref[...], (chunk, z.shape[1]))
            else:
                acc = o_ref[0, c * chunk:(c + 1) * chunk, :]
            for k in taps:
                r0 = c * chunk + (pad - half + k) - s
                acc = acc + w_ref[k:k + 1, :] * zsh_ref[r0:r0 + chunk, :]
            o_ref[0, c * chunk:(c + 1) * chunk, :] = acc


def _cf_conv(cf_in, w, bias):
    b, l, w2 = cf_in.shape
    wd = w2 // 2
    cb = CONV_CB
    nb = wd // cb
    chunk = min(256, l)
    return pl.pallas_call(
        functools.partial(_cf_conv_kernel, chunk=chunk),
        grid=(b, nb),
        in_specs=[pl.BlockSpec((1, l, cb), lambda bi, ci: (bi, 0, ci)),
                  pl.BlockSpec((1, l, cb), lambda bi, ci: (bi, 0, ci + nb)),
                  pl.BlockSpec((CF_K, cb), lambda bi, ci: (0, ci)),
                  pl.BlockSpec((1, cb), lambda bi, ci: (0, ci))],
        out_specs=pl.BlockSpec((1, l, cb), lambda bi, ci: (bi, 0, ci)),
        out_shape=jax.ShapeDtypeStruct((b, l, wd), F32),
        scratch_shapes=[pltpu.VMEM((l + 32, cb), F32), pltpu.VMEM((l + 24, cb), F32)],
        compiler_params=_cparams(2, 32),
        name="cf_conv",
    )(cf_in, cf_in, w.astype(F32), bias.astype(F32).reshape(1, wd))


def _merge_kernel(ya_ref, yh_ref, yc_ref, ga_ref, gh_ref, gc_ref, h_ref, wa_ref, wh_ref, wc_ref,
                  wo_ref, bo_ref, cg_ref, cb_ref, g1_ref, b1_ref, o_ref, *, alpha):
    yc = _layer_norm(yc_ref[...], cg_ref[...], cb_ref[...])
    yc = (yc * _sigmoid(yc)).astype(BF16)
    m = _sigmoid(ga_ref[...].astype(F32)) * jnp.dot(ya_ref[...], wa_ref[...], preferred_element_type=F32)
    m = m + _sigmoid(gh_ref[...].astype(F32)) * jnp.dot(yh_ref[...], wh_ref[...], preferred_element_type=F32)
    m = m + _sigmoid(gc_ref[...].astype(F32)) * jnp.dot(yc, wc_ref[...], preferred_element_type=F32)
    mix = jnp.dot(m.astype(BF16), wo_ref[...], preferred_element_type=F32) + bo_ref[...]
    o_ref[...] = _layer_norm(alpha * h_ref[...] + mix, g1_ref[...], b1_ref[...])


def _merge(ya, yh, yc, gates, h, wa, wh, wc, wo, bo, cg, cb, g1, b1, layer, alpha):
    n, d = h.shape
    w = ya.shape[1]
    tm = ROW_TILE
    row = lambda c: pl.BlockSpec((tm, c), lambda i: (i, 0))
    gate = lambda j: pl.BlockSpec((tm, d), lambda i: (i, j))
    once = pl.Buffered(1)
    wspec = lambda r, c: pl.BlockSpec((None, r, c), lambda i: (layer, 0, 0), pipeline_mode=once)
    vec = lambda c: pl.BlockSpec((None, 1, c), lambda i: (layer, 0, 0))
    return pl.pallas_call(
        functools.partial(_merge_kernel, alpha=alpha),
        grid=(n // tm,),
        in_specs=[row(w), row(w), row(w), gate(0), gate(1), gate(2), row(d),
                  wspec(w, d), wspec(w, d), wspec(w, d), wspec(d, d),
                  vec(d), vec(w), vec(w), vec(d), vec(d)],
        out_specs=row(d),
        out_shape=jax.ShapeDtypeStruct((n, d), F32),
        compiler_params=_cparams(1, 48),
        name="merge",
    )(ya, yh, yc, gates, gates, gates, h, wa, wh, wc, wo, bo, cg, cb, g1, b1)


def _first_max(vals):
    m = vals[0]
    for v in vals[1:]:
        m = jnp.maximum(m, v)
    idx = jnp.full(m.shape, len(vals) - 1, jnp.int32)
    for j in range(len(vals) - 2, -1, -1):
        idx = jnp.where(vals[j] == m, j, idx)
    return m, idx


def _top2(vals):
    m1, i1 = _first_max(vals)
    rest = [jnp.where(i1 == j, -1.0, v) for j, v in enumerate(vals)]
    m2, i2 = _first_max(rest)
    return m1, i1, m2, i2


def _router_kernel(h_ref, wr_ref, br_ref, idx_ref, wgt_ref, cnt_ref, carry_ref):
    i = pl.program_id(0)

    @pl.when(i == 0)
    def _():
        carry_ref[...] = jnp.zeros_like(carry_ref)

    logits = jnp.dot(h_ref[...], wr_ref[...], precision=lax.Precision.HIGHEST,
                     preferred_element_type=F32) + br_ref[...]
    lt = logits.T[:N_EXPERTS]
    tm = lt.shape[1]
    mx = jnp.max(lt, axis=0, keepdims=True)
    ex = jnp.exp(lt - mx)
    probs = ex / jnp.sum(ex, axis=0, keepdims=True)
    p = [probs[e:e + 1, :] for e in range(N_EXPERTS)]
    scores = []
    for g in range(N_GROUPS):
        a, _, b, _ = _top2(p[g * EXPERTS_PER_GROUP:(g + 1) * EXPERTS_PER_GROUP])
        scores.append(a + b)
    _, g_sel = _first_max(scores)
    pg = []
    for j in range(EXPERTS_PER_GROUP):
        v = p[(N_GROUPS - 1) * EXPERTS_PER_GROUP + j]
        for g in range(N_GROUPS - 2, -1, -1):
            v = jnp.where(g_sel == g, p[g * EXPERTS_PER_GROUP + j], v)
        pg.append(v)
    p1, i1, p2, i2 = _top2(pg)
    den = p1 + p2
    e0 = g_sel * EXPERTS_PER_GROUP + i1
    e1 = g_sel * EXPERTS_PER_GROUP + i2

    erow = lax.broadcasted_iota(jnp.int32, (N_EXPERTS, tm), 0)
    oh0 = (erow == e0).astype(F32)
    oh1 = (erow == e1).astype(F32)
    both = oh0 + oh1
    before = (lax.broadcasted_iota(jnp.int32, (tm, tm), 0) < lax.broadcasted_iota(jnp.int32, (tm, tm), 1))
    cum = jnp.dot(both.astype(BF16), before.astype(BF16), preferred_element_type=F32) + carry_ref[:, 0:1]
    r0 = jnp.sum(oh0 * cum, axis=0, keepdims=True)
    r1 = jnp.sum(oh1 * cum, axis=0, keepdims=True)
    carry_ref[...] = carry_ref[...] + jnp.sum(both, axis=1, keepdims=True)
    cnt_ref[...] = carry_ref[...]

    zi = jnp.zeros((4, tm), jnp.int32)
    idx_ref[...] = jnp.concatenate([e0, e1, r0.astype(jnp.int32), r1.astype(jnp.int32), zi], axis=0)
    zf = jnp.zeros((6, tm), F32)
    wgt_ref[...] = jnp.concatenate([p1 / den, p2 / den, zf], axis=0)


def _router(h, w_router, b_router):
    n, d = h.shape
    tm = ROW_TILE
    wr = _pad2(w_router, d, LANES)
    br = _pad2(b_router[None], 1, LANES)
    return pl.pallas_call(
        _router_kernel,
        grid=(n // tm,),
        in_specs=[pl.BlockSpec((tm, d), lambda i: (i, 0)),
                  pl.BlockSpec((d, LANES), lambda i: (0, 0)),
                  pl.BlockSpec((1, LANES), lambda i: (0, 0))],
        out_specs=[pl.BlockSpec((8, tm), lambda i: (0, i)),
                   pl.BlockSpec((8, tm), lambda i: (0, i)),
                   pl.BlockSpec((N_EXPERTS, LANES), lambda i: (0, 0))],
        out_shape=[jax.ShapeDtypeStruct((8, n), jnp.int32),
                   jax.ShapeDtypeStruct((8, n), F32),
                   jax.ShapeDtypeStruct((N_EXPERTS, LANES), F32)],
        scratch_shapes=[pltpu.VMEM((N_EXPERTS, LANES), F32)],
        compiler_params=_cparams(1, 32),
        name="router",
    )(h, wr, br)


def _expert_kernel(texp_ref, nv_ref, stok_ref, h_hbm, wg_ref, wu_ref, wd_ref, o_ref, xbuf, sem, *, tm):
    i = pl.program_id(0)
    nv = nv_ref[0]

    def issue(tile, slot):
        def body(r, carry):
            tok = stok_ref[tile * tm + r]
            pltpu.make_async_copy(h_hbm.at[pl.ds(tok, 1), :], xbuf.at[slot, pl.ds(r, 1), :],
                                  sem.at[slot]).start()
            return carry
        lax.fori_loop(0, tm, body, 0, unroll=GATHER_UNROLL)

    @pl.when(i == 0)
    def _():
        issue(0, 0)

    @pl.when(i + 1 < nv)
    def _():
        issue(i + 1, (i + 1) % 2)

    @pl.when(i < nv)
    def _():
        slot = i % 2
        pltpu.make_async_copy(h_hbm.at[pl.ds(0, tm), :], xbuf.at[slot], sem.at[slot]).wait()
        x = xbuf[slot].astype(BF16)
        g = jnp.dot(x, wg_ref[...], preferred_element_type=F32)
        u = jnp.dot(x, wu_ref[...], preferred_element_type=F32)
        hid = (g * _sigmoid(g) * u).astype(BF16)
        o_ref[...] = jnp.dot(hid, wd_ref[...], preferred_element_type=F32)

    @pl.when(i >= nv)
    def _():
        o_ref[...] = jnp.zeros_like(o_ref)


def _experts(h, texp, nvalid, slot_tok, wg, wu, wd, layer):
    n, d = h.shape
    de = wg.shape[-1]
    tm = EXPERT_TM
    n_tiles = slot_tok.shape[0] // tm
    grid_spec = pltpu.PrefetchScalarGridSpec(
        num_scalar_prefetch=3,
        grid=(n_tiles,),
        in_specs=[pl.BlockSpec(memory_space=pl.ANY),
                  pl.BlockSpec((None, None, d, de), lambda i, te, nv, st: (layer, te[i], 0, 0)),
                  pl.BlockSpec((None, None, d, de), lambda i, te, nv, st: (layer, te[i], 0, 0)),
                  pl.BlockSpec((None, None, de, d), lambda i, te, nv, st: (layer, te[i], 0, 0))],
        out_specs=pl.BlockSpec((tm, d), lambda i, te, nv, st: (i, 0)),
        scratch_shapes=[pltpu.VMEM((2, tm, d), F32), pltpu.SemaphoreType.DMA((2,))],
    )
    return pl.pallas_call(
        functools.partial(_expert_kernel, tm=tm),
        grid_spec=grid_spec,
        out_shape=jax.ShapeDtypeStruct((n_tiles * tm, d), F32),
        compiler_params=_cparams(1, 56, row_gather=True),
        name="experts",
    )(texp, nvalid, slot_tok, h, wg, wu, wd)


def _combine_kernel(p0_ref, p1_ref, ys_hbm, h_ref, w0_ref, w1_ref, g_ref, b_ref, o_ref, ob_ref,
                    gbuf, sem, *, tm, alpha):
    i = pl.program_id(0)
    nt = pl.num_programs(0)

    def issue(tile, slot):
        def body(r, carry):
            t = tile * tm + r
            pltpu.make_async_copy(ys_hbm.at[pl.ds(p0_ref[t], 1), :], gbuf.at[slot, 0, pl.ds(r, 1), :],
                                  sem.at[slot]).start()
            pltpu.make_async_copy(ys_hbm.at[pl.ds(p1_ref[t], 1), :], gbuf.at[slot, 1, pl.ds(r, 1), :],
                                  sem.at[slot]).start()
            return carry
        lax.fori_loop(0, tm, body, 0, unroll=GATHER_UNROLL)

    @pl.when(i == 0)
    def _():
        issue(0, 0)

    @pl.when(i + 1 < nt)
    def _():
        issue(i + 1, (i + 1) % 2)

    slot = i % 2
    for j in range(2):
        pltpu.make_async_copy(ys_hbm.at[pl.ds(0, tm), :], gbuf.at[slot, j], sem.at[slot]).wait()
    y = w0_ref[...] * gbuf[slot, 0] + w1_ref[...] * gbuf[slot, 1]
    out = _layer_norm(alpha * h_ref[...] + y, g_ref[...], b_ref[...])
    o_ref[...] = out
    ob_ref[...] = out.astype(BF16)


def _combine(ys, h, pos0, pos1, w0, w1, g2, b2, layer, alpha):
    n, d = h.shape
    tm = ROW_TILE
    row = pl.BlockSpec((tm, d), lambda i, a, b: (i, 0))
    col = pl.BlockSpec((tm, 1), lambda i, a, b: (i, 0))
    vec = pl.BlockSpec((None, 1, d), lambda i, a, b: (layer, 0, 0))
    grid_spec = pltpu.PrefetchScalarGridSpec(
        num_scalar_prefetch=2,
        grid=(n // tm,),
        in_specs=[pl.BlockSpec(memory_space=pl.ANY), row, col, col, vec, vec],
        out_specs=[row, row],
        scratch_shapes=[pltpu.VMEM((2, 2, tm, d), F32), pltpu.SemaphoreType.DMA((2,))],
    )
    return pl.pallas_call(
        functools.partial(_combine_kernel, tm=tm, alpha=alpha),
        grid_spec=grid_spec,
        out_shape=[jax.ShapeDtypeStruct((n, d), F32), jax.ShapeDtypeStruct((n, d), BF16)],
        compiler_params=_cparams(1, 40, row_gather=True),
        name="combine",
    )(pos0, pos1, ys, h, w0, w1, g2, b2)


def _routing_tables(idx, cnt, n):
    tm = EXPERT_TM
    e0, e1, r0, r1 = idx[0], idx[1], idx[2], idx[3]
    counts = cnt[:, 0].astype(jnp.int32)
    padded = ((counts + tm - 1) // tm) * tm
    pend = jnp.cumsum(padded)
    poff = pend - padded
    eid = jnp.arange(N_EXPERTS, dtype=jnp.int32)[:, None]
    pos0 = jnp.sum(jnp.where(e0[None, :] == eid, poff[:, None], 0), axis=0) + r0
    pos1 = jnp.sum(jnp.where(e1[None, :] == eid, poff[:, None], 0), axis=0) + r1
    n_slots = 2 * n + N_EXPERTS * tm
    tok = jnp.arange(n, dtype=jnp.int32)
    slot_tok = jnp.zeros((n_slots,), jnp.int32).at[pos0].set(tok).at[pos1].set(tok)
    n_tiles = n_slots // tm
    nvalid = pend[-1] // tm
    tile = jnp.arange(n_tiles, dtype=jnp.int32)
    start = jnp.minimum(tile, nvalid - 1) * tm
    texp = jnp.sum((start[:, None] >= pend[None, :]).astype(jnp.int32), axis=1)
    texp = jnp.minimum(texp, N_EXPERTS - 1)
    return pos0, pos1, slot_tok, texp, nvalid.reshape(1).astype(jnp.int32)


def kernel(x, in_ln_g, in_ln_b, w_in, b_in, attn_rpb, hy_conv_w, hy_conv_b, hy_f_w1, hy_f_b1, hy_f_w2, hy_f_b2, hy_f_w3, hy_f_b3, hy_f_freq, hy_f_w4, hy_skip, cf_dw_w, cf_dw_b, cf_ln_g, cf_ln_b, w_attn_br, w_hy_br, w_cf_br, w_o, b_o, ln1_g, ln1_b, w_router, b_router, moe_w_gate, moe_w_up, moe_w_down, ln2_g, ln2_b):
    bsz, l, d = x.shape
    depth = w_in.shape[0]
    n = bsz * l
    mw = w_attn_br.shape[1]
    alpha = (2 * depth) ** 0.25
    c_qkv, c_hy, c_cf, c_gate = 0, 3 * mw, 6 * mw, 8 * mw

    w_in_b = w_in.astype(BF16)
    b_in3 = b_in.astype(F32)[:, None, :]
    wa_b, wh_b, wc_b, wo_b = (w.astype(BF16) for w in (w_attn_br, w_hy_br, w_cf_br, w_o))
    wg_b, wu_b, wd_b = (w.astype(BF16) for w in (moe_w_gate, moe_w_up, moe_w_down))
    vec3 = lambda v: v.astype(F32)[:, None, :]
    bo3, cg3, cb3, g13, b13, g23, b23 = map(vec3, (b_o, cf_ln_g, cf_ln_b, ln1_g, ln1_b, ln2_g, ln2_b))
    fwd_np, inv_np = _dft_matrices(l, DFT_TK)
    fwd, inv = jnp.asarray(fwd_np), jnp.asarray(inv_np)

    h, hb = _in_ln(x.reshape(n, d), in_ln_g, in_ln_b)
    for layer in range(depth):
        proj = lambda c0, nc, dt, nm: _matmul(hb, w_in_b, b_in3, col0=c0, ncols=nc, out_dtype=dt,
                                              layer=layer, name=nm)
        qkv = proj(c_qkv, 3 * mw, BF16, "proj_qkv")
        hy_in = proj(c_hy, 3 * mw, F32, "proj_hyena")
        cf_in = proj(c_cf, 2 * mw, F32, "proj_conformer")
        gates = proj(c_gate, 3 * d, BF16, "proj_gates")

        y_a = _natten(qkv.reshape(bsz, l, 3 * mw), _natten_bias_table(attn_rpb[layer]))

        taps = _hy_filters(l, hy_f_w1[layer], hy_f_b1[layer], hy_f_w2[layer], hy_f_b2[layer],
                           hy_f_w3[layer], hy_f_b3[layer], hy_f_freq[layer], hy_f_w4[layer])
        spec = _matmul(fwd, taps, None, col0=0, ncols=taps.shape[1], out_dtype=F32, name="hy_spectrum")
        uc = _short_conv(hy_in.reshape(bsz, l, 3 * mw), hy_conv_w[layer], hy_conv_b[layer])
        z1 = _long_conv(uc, 0, uc, 1, hy_skip[layer, 0], spec, 0, fwd, inv)
        y_h = _long_conv(z1, 0, uc, 2, hy_skip[layer, 1], spec, 1, fwd, inv)

        y_c = _cf_conv(cf_in.reshape(bsz, l, 2 * mw), cf_dw_w[layer], cf_dw_b[layer])

        h = _merge(y_a.reshape(n, mw), y_h.reshape(n, mw), y_c.reshape(n, mw), gates, h,
                   wa_b, wh_b, wc_b, wo_b, bo3, cg3, cb3, g13, b13, layer, alpha)

        idx, wgt, cnt = _router(h, w_router, b_router)
        pos0, pos1, slot_tok, texp, nvalid = _routing_tables(idx, cnt, n)
        ys = _experts(h, texp, nvalid, slot_tok, wg_b, wu_b, wd_b, layer)
        h, hb = _combine(ys, h, pos0, pos1, wgt[0].reshape(n, 1), wgt[1].reshape(n, 1),
                         g23, b23, layer, alpha)
    return h.reshape(bsz, l, d)
```

```python
import functools
import math

import numpy as np
import jax
import jax.numpy as jnp
from jax import lax
from jax.experimental import pallas as pl
from jax.experimental.pallas import tpu as pltpu

F32 = jnp.float32
BF16 = jnp.bfloat16

GRID_W = 64
NA_HEAD_DIM = 64
NA_KH = 8
NA_KW = 16
HY_ORDER = 2
HY_POS_DIM = 33
HY_FAST_DECAY = 0.3
HY_SLOW_DECAY = 1.5
HY_DECAY_TARGET = 1e-2
CF_K = 31
N_GROUPS = 4
EXPERTS_PER_GROUP = 4
N_EXPERTS = N_GROUPS * EXPERTS_PER_GROUP
LN_EPS = 1e-5
NEG_INF = -1e30

LANES = 128
V7X_VMEM_BYTES = 64 * 1024 * 1024
MIB = 1024 * 1024

ROW_TILE = 256
MM_TM = 1024
MM_TN = 768
DFT_TK = 256
CONV_CB = 256
EXPERT_TM = 512
GATHER_UNROLL = 8
NATTEN_BATCH = 8


def _cparams(n_axes, vmem_mib, row_gather=False):
    assert vmem_mib * MIB < V7X_VMEM_BYTES
    return pltpu.CompilerParams(dimension_semantics=("arbitrary",) * n_axes,
                                vmem_limit_bytes=vmem_mib * MIB,
                                disable_bounds_checks=row_gather)


def _layer_norm(x, g, b):
    mu = jnp.mean(x, axis=-1, keepdims=True)
    xc = x - mu
    var = jnp.mean(xc * xc, axis=-1, keepdims=True)
    return xc * lax.rsqrt(var + LN_EPS) * g + b


def _sigmoid(x):
    return 1.0 / (1.0 + jnp.exp(-x))


def _in_ln_kernel(x_ref, g_ref, b_ref, h_ref, hb_ref):
    y = _layer_norm(x_ref[...], g_ref[...], b_ref[...])
    h_ref[...] = y
    hb_ref[...] = y.astype(BF16)


def _in_ln(x2, g, b):
    n, d = x2.shape
    row = pl.BlockSpec((ROW_TILE, d), lambda i: (i, 0))
    vec = pl.BlockSpec((1, d), lambda i: (0, 0))
    return pl.pallas_call(
        _in_ln_kernel,
        grid=(n // ROW_TILE,),
        in_specs=[row, vec, vec],
        out_specs=[row, row],
        out_shape=[jax.ShapeDtypeStruct((n, d), F32), jax.ShapeDtypeStruct((n, d), BF16)],
        compiler_params=_cparams(1, 32),
        name="in_ln",
    )(x2, g.reshape(1, d), b.reshape(1, d))


def _mm_bias_kernel(x_ref, w_ref, b_ref, o_ref):
    acc = jnp.dot(x_ref[...], w_ref[...], preferred_element_type=F32)
    o_ref[...] = (acc + b_ref[...]).astype(o_ref.dtype)


def _mm_kernel(x_ref, w_ref, o_ref):
    o_ref[...] = jnp.dot(x_ref[...], w_ref[...], preferred_element_type=F32).astype(o_ref.dtype)


def _matmul(x, w, bias, *, col0, ncols, out_dtype, layer=None, name):
    m, k = x.shape
    tm = min(MM_TM, m)
    tn = MM_TN
    assert m % tm == 0 and ncols % tn == 0 and col0 % tn == 0
    jb = col0 // tn
    x_spec = pl.BlockSpec((tm, k), lambda i, j: (i, 0))
    if layer is None:
        w_spec = pl.BlockSpec((k, tn), lambda i, j: (0, j + jb))
    else:
        w_spec = pl.BlockSpec((None, k, tn), lambda i, j: (layer, 0, j + jb))
    o_spec = pl.BlockSpec((tm, tn), lambda i, j: (i, j))
    args, specs, body = [x, w], [x_spec, w_spec], _mm_kernel
    if bias is not None:
        if layer is None:
            specs.append(pl.BlockSpec((1, tn), lambda i, j: (0, j + jb)))
        else:
            specs.append(pl.BlockSpec((None, 1, tn), lambda i, j: (layer, 0, j + jb)))
        args.append(bias)
        body = _mm_bias_kernel
    return pl.pallas_call(
        body,
        grid=(m // tm, ncols // tn),
        in_specs=specs,
        out_specs=o_spec,
        out_shape=jax.ShapeDtypeStruct((m, ncols), out_dtype),
        compiler_params=_cparams(2, 40),
        name=name,
    )(*args)


def _natten_bias_table(rpb):
    qc = np.arange(GRID_W)
    kc = np.arange(GRID_W)
    cs = np.clip(qc - NA_KW // 2, 0, GRID_W - NA_KW)
    valid = (kc[None, :] >= cs[:, None]) & (kc[None, :] < cs[:, None] + NA_KW)
    col_off = np.clip(kc[None, :] - qc[:, None], -(NA_KW - 1), NA_KW - 1) + (NA_KW - 1)
    row_off = np.arange(NA_KH)[None, :] - np.arange(NA_KH)[:, None] + (NA_KH - 1)
    row_sel = (row_off[:, :, None] == np.arange(2 * NA_KH - 1)).astype(np.float32)
    col_sel = (col_off[:, :, None] == np.arange(2 * NA_KW - 1)).astype(np.float32)
    b = jnp.einsum("hrc,djr,qkc->dhqjk", rpb.astype(F32), row_sel, col_sel,
                   precision=lax.Precision.HIGHEST)
    b = jnp.where(jnp.asarray(valid)[None, None, :, None, :], b, NEG_INF)
    h = rpb.shape[0]
    return b.reshape(NA_KH, h, GRID_W, NA_KH * GRID_W)


def _natten_kernel(q_ref, k_ref, v_ref, bias_ref, o_ref, s_ref, p_ref, *, rows, batch):
    win = NA_KH * GRID_W
    pair_rows = 2 * GRID_W
    lane = lax.broadcasted_iota(jnp.int32, (GRID_W, 2 * NA_HEAD_DIM), 1)
    first = lane < NA_HEAD_DIM
    scale = jnp.asarray(NA_HEAD_DIM ** -0.5, BF16)
    window_start = lambda r: min(max(r - NA_KH // 2, 0), rows - NA_KH)
    for r0 in range(0, rows, batch):
        for i in range(batch):
            r = r0 + i
            rs = window_start(r)
            q = q_ref[0, r * GRID_W:(r + 1) * GRID_W, :] * scale
            zero = jnp.zeros_like(q)
            qm = jnp.concatenate([jnp.where(first, q, zero), jnp.where(first, zero, q)], axis=0)
            kw = k_ref[0, rs * GRID_W:rs * GRID_W + win, :]
            s = lax.dot_general(qm, kw, (((1,), (1,)), ((), ())), preferred_element_type=F32)
            s_ref[i * pair_rows:(i + 1) * pair_rows, :] = s + bias_ref[r - rs].reshape(pair_rows, win)
        s = s_ref[...]
        e = jnp.exp(s - jnp.max(s, axis=-1, keepdims=True))
        p_ref[...] = (e / jnp.sum(e, axis=-1, keepdims=True)).astype(BF16)
        for i in range(batch):
            r = r0 + i
            rs = window_start(r)
            vw = v_ref[0, rs * GRID_W:rs * GRID_W + win, :]
            o = jnp.dot(p_ref[i * pair_rows:(i + 1) * pair_rows, :], vw, preferred_element_type=F32)
            o_ref[0, r * GRID_W:(r + 1) * GRID_W, :] = (
                jnp.where(first, o[:GRID_W], o[GRID_W:]).astype(o_ref.dtype))


def _natten(qkv, bias_tab):
    b, l, w3 = qkv.shape
    w = w3 // 3
    pair = 2 * NA_HEAD_DIM
    npairs = w // pair
    rows = l // GRID_W
    batch = NATTEN_BATCH
    assert rows >= NA_KH and rows % batch == 0
    blk = lambda off: pl.BlockSpec((1, l, pair), lambda bi, hp: (bi, 0, hp + off))
    return pl.pallas_call(
        functools.partial(_natten_kernel, rows=rows, batch=batch),
        grid=(b, npairs),
        in_specs=[blk(0), blk(npairs), blk(2 * npairs),
                  pl.BlockSpec((NA_KH, 2, GRID_W, NA_KH * GRID_W), lambda bi, hp: (0, hp, 0, 0))],
        out_specs=pl.BlockSpec((1, l, pair), lambda bi, hp: (bi, 0, hp)),
        out_shape=jax.ShapeDtypeStruct((b, l, w), BF16),
        scratch_shapes=[pltpu.VMEM((batch * 2 * GRID_W, NA_KH * GRID_W), F32),
                        pltpu.VMEM((batch * 2 * GRID_W, NA_KH * GRID_W), BF16)],
        compiler_params=_cparams(2, 32),
        name="natten",
    )(qkv, qkv, qkv, bias_tab)


@functools.lru_cache(maxsize=None)
def _dft_matrices(l, tk):
    n2 = 2 * l
    k = np.arange(l, dtype=np.int64)
    n = np.arange(l, dtype=np.int64)
    ang = 2.0 * np.pi * ((k[:, None] * n[None, :]) % n2).astype(np.float64) / n2
    f_re = np.cos(ang)
    f_im = -np.sin(ang)
    f_im[0, :] = np.cos(np.pi * n)
    g_re = (2.0 / n2) * np.cos(ang).T
    g_re[:, 0] = 1.0 / n2
    g_im = -(2.0 / n2) * np.sin(ang).T
    g_im[:, 0] = np.cos(np.pi * n) / n2
    kt = l // tk
    fwd = np.stack([f_re.reshape(kt, tk, l), f_im.reshape(kt, tk, l)], axis=1).reshape(2 * l, l)
    inv = np.stack([g_re.reshape(l, kt, tk), g_im.reshape(l, kt, tk)], axis=2).reshape(l, 2 * l)
    return np.asarray(fwd, dtype=BF16), np.asarray(inv, dtype=BF16)


def _hy_filter_kernel(z_ref, t_ref, dl_ref, w1_ref, b1_ref, w2_ref, b2_ref, w3_ref, b3_ref,
                      fr_ref, w4_ref, o_ref, *, half):
    hp = lax.Precision.HIGHEST
    fr = fr_ref[...]
    h = jnp.sin(fr * (jnp.dot(z_ref[...], w1_ref[...], precision=hp, preferred_element_type=F32) + b1_ref[...]))
    h = jnp.sin(fr * (jnp.dot(h, w2_ref[...], precision=hp, preferred_element_type=F32) + b2_ref[...]))
    h = jnp.sin(fr * (jnp.dot(h, w3_ref[...], precision=hp, preferred_element_type=F32) + b3_ref[...]))
    h = jnp.dot(h, w4_ref[...], precision=hp, preferred_element_type=F32)
    h = h * jnp.exp(-t_ref[...] * dl_ref[...])
    tl, nc = h.shape
    row = lax.broadcasted_iota(jnp.int32, (tl, nc), 0) + pl.program_id(0) * tl
    col = lax.broadcasted_iota(jnp.int32, (tl, nc), 1)
    h = jnp.where((row == 0) & (col >= half), 0.0, h)
    o_ref[...] = h.astype(o_ref.dtype)


def _pad2(a, r, c):
    return jnp.pad(a.astype(F32), ((0, r - a.shape[0]), (0, c - a.shape[1])))


def _hy_filters(l, w1, b1, w2, b2, w3, b3, freq, w4):
    t = jnp.linspace(0.0, 1.0, l, dtype=F32)[:, None]
    bands = (HY_POS_DIM - 1) // 2
    w = 2.0 * math.pi * jnp.arange(l, dtype=F32)[:, None] / l
    f = jnp.linspace(1e-4, bands - 1, bands, dtype=F32)[None, :]
    z = jnp.concatenate([t, jnp.cos(f * w), -jnp.sin(f * w)], axis=-1)
    nc = w4.shape[1]
    max_decay = math.log(HY_DECAY_TARGET) / HY_FAST_DECAY
    min_decay = math.log(HY_DECAY_TARGET) / HY_SLOW_DECAY
    deltas = jnp.abs(jnp.linspace(min_decay, max_decay, nc, dtype=F32))[None, :]
    hid = LANES
    tl = min(256, l)
    full = lambda r, c: pl.BlockSpec((r, c), lambda i: (0, 0))
    return pl.pallas_call(
        functools.partial(_hy_filter_kernel, half=nc // 2),
        grid=(l // tl,),
        in_specs=[pl.BlockSpec((tl, hid), lambda i: (i, 0)), pl.BlockSpec((tl, 1), lambda i: (i, 0)),
                  full(1, nc), full(hid, hid), full(1, hid), full(hid, hid), full(1, hid),
                  full(hid, hid), full(1, hid), full(1, hid), full(hid, nc)],
        out_specs=pl.BlockSpec((tl, nc), lambda i: (i, 0)),
        out_shape=jax.ShapeDtypeStruct((l, nc), BF16),
        compiler_params=_cparams(1, 32),
        name="hy_filter",
    )(_pad2(z, l, hid), t, deltas, _pad2(w1, hid, hid), _pad2(b1[None], 1, hid),
      _pad2(w2, hid, hid), _pad2(b2[None], 1, hid), _pad2(w3, hid, hid), _pad2(b3[None], 1, hid),
      _pad2(freq[None], 1, hid), _pad2(w4, hid, nc))


def _short_conv_kernel(u_ref, w_ref, b_ref, o_ref):
    u = u_ref[0]
    l = u.shape[0]
    row = lax.broadcasted_iota(jnp.int32, u.shape, 0)
    prev = jnp.where(row == 0, 0.0, pltpu.roll(u, 1, 0))
    nxt = jnp.where(row == l - 1, 0.0, pltpu.roll(u, l - 1, 0))
    out = w_ref[0:1, :] * prev + w_ref[1:2, :] * u + w_ref[2:3, :] * nxt + b_ref[...]
    o_ref[0] = out.astype(o_ref.dtype)


def _short_conv(u, w, bias):
    b, l, c = u.shape
    cb = CONV_CB
    return pl.pallas_call(
        _short_conv_kernel,
        grid=(b, c // cb),
        in_specs=[pl.BlockSpec((1, l, cb), lambda bi, ci: (bi, 0, ci)),
                  pl.BlockSpec((3, cb), lambda bi, ci: (0, ci)),
                  pl.BlockSpec((1, cb), lambda bi, ci: (0, ci))],
        out_specs=pl.BlockSpec((1, l, cb), lambda bi, ci: (bi, 0, ci)),
        out_shape=jax.ShapeDtypeStruct((b, l, c), BF16),
        compiler_params=_cparams(2, 32),
        name="hy_short_conv",
    )(u, w.astype(F32), bias.astype(F32).reshape(1, c))


def _long_conv_kernel(x_ref, gate_ref, skip_ref, f_ref, g_ref, hf_ref, hb_ref, o_ref, acc_ref, *, tk):
    kt = pl.program_id(1)

    @pl.when(kt == 0)
    def _():
        acc_ref[...] = jnp.zeros_like(acc_ref)

    xb = x_ref[0]
    z = jnp.dot(f_ref[...], xb, preferred_element_type=F32)
    zr, zi = z[:tk], z[tk:]
    hf = hf_ref[...]
    hb = hb_ref[...]
    hr = hf[:tk] + hb[:tk]
    hi = hf[tk:] - hb[tk:]
    hny = hf[tk:] + hb[tk:]
    row0 = (lax.broadcasted_iota(jnp.int32, zr.shape, 0) == 0) & (kt == 0)
    yr = zr * hr - jnp.where(row0, 0.0, zi * hi)
    yi = jnp.where(row0, zi * hny, zr * hi + zi * hr)
    y = jnp.concatenate([yr, yi], axis=0).astype(BF16)
    acc_ref[...] += jnp.dot(g_ref[...], y, preferred_element_type=F32)

    @pl.when(kt == pl.num_programs(1) - 1)
    def _():
        o_ref[0] = (gate_ref[0].astype(F32) * (acc_ref[...] + xb.astype(F32) * skip_ref[...])).astype(o_ref.dtype)


def _long_conv(x_arr, x_blk, gate_arr, gate_blk, skip, spec, order, fwd, inv):
    b, l, _ = x_arr.shape
    c = skip.shape[-1]
    tk = DFT_TK
    kt = l // tk
    return pl.pallas_call(
        functools.partial(_long_conv_kernel, tk=tk),
        grid=(b, kt),
        in_specs=[pl.BlockSpec((1, l, c), lambda bi, ki: (bi, 0, x_blk)),
                  pl.BlockSpec((1, l, c), lambda bi, ki: (bi, 0, gate_blk)),
                  pl.BlockSpec((1, c), lambda bi, ki: (0, 0)),
                  pl.BlockSpec((2 * tk, l), lambda bi, ki: (ki, 0)),
                  pl.BlockSpec((l, 2 * tk), lambda bi, ki: (0, ki)),
                  pl.BlockSpec((2 * tk, c), lambda bi, ki: (ki, order)),
                  pl.BlockSpec((2 * tk, c), lambda bi, ki: (ki, HY_ORDER + order))],
        out_specs=pl.BlockSpec((1, l, c), lambda bi, ki: (bi, 0, 0)),
        out_shape=jax.ShapeDtypeStruct((b, l, c), BF16),
        scratch_shapes=[pltpu.VMEM((l, c), F32)],
        compiler_params=_cparams(2, 52),
        name=f"hy_long_conv{order}",
    )(x_arr, gate_arr, skip.astype(F32).reshape(1, c), fwd, inv, spec, spec)


def _cf_conv_kernel(a_ref, g_ref, w_ref, b_ref, o_ref, zpad_ref, zsh_ref, *, chunk):
    l = a_ref.shape[1]
    pad = 16
    sub = 8
    z = a_ref[0] * _sigmoid(g_ref[0])
    zeros = jnp.zeros((pad, z.shape[1]), F32)
    zpad_ref[0:pad, :] = zeros
    zpad_ref[pad:pad + l, :] = z
    zpad_ref[pad + l:pad + l + pad, :] = zeros
    half = CF_K // 2
    nsh = zsh_ref.shape[0]
    for s in range(sub):
        zsh_ref[...] = zpad_ref[s:s + nsh, :]
        taps = [k for k in range(CF_K) if (pad - half + k) % sub == s]
        for c in range(l // chunk):
            if s == 0:
                acc = jnp.broadcast_to(b_ref[...], (chunk, z.shape[1]))
            else:
                acc = o_ref[0, c * chunk:(c + 1) * chunk, :]
            for k in taps:
                r0 = c * chunk + (pad - half + k) - s
                acc = acc + w_ref[k:k + 1, :] * zsh_ref[r0:r0 + chunk, :]
            o_ref[0, c * chunk:(c + 1) * chunk, :] = acc


def _cf_conv(cf_in, w, bias):
    b, l, w2 = cf_in.shape
    wd = w2 // 2
    cb = CONV_CB
    nb = wd // cb
    chunk = min(256, l)
    return pl.pallas_call(
        functools.partial(_cf_conv_kernel, chunk=chunk),
        grid=(b, nb),
        in_specs=[pl.BlockSpec((1, l, cb), lambda bi, ci: (bi, 0, ci)),
                  pl.BlockSpec((1, l, cb), lambda bi, ci: (bi, 0, ci + nb)),
                  pl.BlockSpec((CF_K, cb), lambda bi, ci: (0, ci)),
                  pl.BlockSpec((1, cb), lambda bi, ci: (0, ci))],
        out_specs=pl.BlockSpec((1, l, cb), lambda bi, ci: (bi, 0, ci)),
        out_shape=jax.ShapeDtypeStruct((b, l, wd), F32),
        scratch_shapes=[pltpu.VMEM((l + 32, cb), F32), pltpu.VMEM((l + 24, cb), F32)],
        compiler_params=_cparams(2, 32),
        name="cf_conv",
    )(cf_in, cf_in, w.astype(F32), bias.astype(F32).reshape(1, wd))


def _merge_kernel(ya_ref, yh_ref, yc_ref, ga_ref, gh_ref, gc_ref, h_ref, wa_ref, wh_ref, wc_ref,
                  wo_ref, bo_ref, cg_ref, cb_ref, g1_ref, b1_ref, o_ref, *, alpha):
    yc = _layer_norm(yc_ref[...], cg_ref[...], cb_ref[...])
    yc = (yc * _sigmoid(yc)).astype(BF16)
    m = _sigmoid(ga_ref[...].astype(F32)) * jnp.dot(ya_ref[...], wa_ref[...], preferred_element_type=F32)
    m = m + _sigmoid(gh_ref[...].astype(F32)) * jnp.dot(yh_ref[...], wh_ref[...], preferred_element_type=F32)
    m = m + _sigmoid(gc_ref[...].astype(F32)) * jnp.dot(yc, wc_ref[...], preferred_element_type=F32)
    mix = jnp.dot(m.astype(BF16), wo_ref[...], preferred_element_type=F32) + bo_ref[...]
    o_ref[...] = _layer_norm(alpha * h_ref[...] + mix, g1_ref[...], b1_ref[...])


def _merge(ya, yh, yc, gates, h, wa, wh, wc, wo, bo, cg, cb, g1, b1, layer, alpha):
    n, d = h.shape
    w = ya.shape[1]
    tm = ROW_TILE
    row = lambda c: pl.BlockSpec((tm, c), lambda i: (i, 0))
    gate = lambda j: pl.BlockSpec((tm, d), lambda i: (i, j))
    once = pl.Buffered(1)
    wspec = lambda r, c: pl.BlockSpec((None, r, c), lambda i: (layer, 0, 0), pipeline_mode=once)
    vec = lambda c: pl.BlockSpec((None, 1, c), lambda i: (layer, 0, 0))
    return pl.pallas_call(
        functools.partial(_merge_kernel, alpha=alpha),
        grid=(n // tm,),
        in_specs=[row(w), row(w), row(w), gate(0), gate(1), gate(2), row(d),
                  wspec(w, d), wspec(w, d), wspec(w, d), wspec(d, d),
                  vec(d), vec(w), vec(w), vec(d), vec(d)],
        out_specs=row(d),
        out_shape=jax.ShapeDtypeStruct((n, d), F32),
        compiler_params=_cparams(1, 48),
        name="merge",
    )(ya, yh, yc, gates, gates, gates, h, wa, wh, wc, wo, bo, cg, cb, g1, b1)


def _first_max(vals):
    m = vals[0]
    for v in vals[1:]:
        m = jnp.maximum(m, v)
    idx = jnp.full(m.shape, len(vals) - 1, jnp.int32)
    for j in range(len(vals) - 2, -1, -1):
        idx = jnp.where(vals[j] == m, j, idx)
    return m, idx


def _top2(vals):
    m1, i1 = _first_max(vals)
    rest = [jnp.where(i1 == j, -1.0, v) for j, v in enumerate(vals)]
    m2, i2 = _first_max(rest)
    return m1, i1, m2, i2


def _router_kernel(h_ref, wr_ref, br_ref, idx_ref, wgt_ref, cnt_ref, carry_ref):
    i = pl.program_id(0)

    @pl.when(i == 0)
    def _():
        carry_ref[...] = jnp.zeros_like(carry_ref)

    h = h_ref[...]
    h_hi = h.astype(BF16)
    h_lo = (h - h_hi.astype(F32)).astype(BF16)
    w = wr_ref[...]
    w_hi = w.astype(BF16)
    w_lo = (w - w_hi.astype(F32)).astype(BF16)
    logits = (jnp.dot(h_hi, w_hi, preferred_element_type=F32) + jnp.dot(h_hi, w_lo, preferred_element_type=F32)
              + jnp.dot(h_lo, w_hi, preferred_element_type=F32)) + br_ref[...]
    lt = logits.T[:N_EXPERTS]
    tm = lt.shape[1]
    mx = jnp.max(lt, axis=0, keepdims=True)
    ex = jnp.exp(lt - mx)
    probs = ex / jnp.sum(ex, axis=0, keepdims=True)
    p = [probs[e:e + 1, :] for e in range(N_EXPERTS)]
    scores = []
    for g in range(N_GROUPS):
        a, _, b, _ = _top2(p[g * EXPERTS_PER_GROUP:(g + 1) * EXPERTS_PER_GROUP])
        scores.append(a + b)
    _, g_sel = _first_max(scores)
    pg = []
    for j in range(EXPERTS_PER_GROUP):
        v = p[(N_GROUPS - 1) * EXPERTS_PER_GROUP + j]
        for g in range(N_GROUPS - 2, -1, -1):
            v = jnp.where(g_sel == g, p[g * EXPERTS_PER_GROUP + j], v)
        pg.append(v)
    p1, i1, p2, i2 = _top2(pg)
    den = p1 + p2
    e0 = g_sel * EXPERTS_PER_GROUP + i1
    e1 = g_sel * EXPERTS_PER_GROUP + i2

    erow = lax.broadcasted_iota(jnp.int32, (N_EXPERTS, tm), 0)
    oh0 = (erow == e0).astype(F32)
    oh1 = (erow == e1).astype(F32)
    both = oh0 + oh1
    before = (lax.broadcasted_iota(jnp.int32, (tm, tm), 0) < lax.broadcasted_iota(jnp.int32, (tm, tm), 1))
    cum = jnp.dot(both.astype(BF16), before.astype(BF16), preferred_element_type=F32) + carry_ref[:, 0:1]
    r0 = jnp.sum(oh0 * cum, axis=0, keepdims=True)
    r1 = jnp.sum(oh1 * cum, axis=0, keepdims=True)
    carry_ref[...] = carry_ref[...] + jnp.sum(both, axis=1, keepdims=True)
    cnt_ref[...] = carry_ref[...]

    zi = jnp.zeros((4, tm), jnp.int32)
    idx_ref[...] = jnp.concatenate([e0, e1, r0.astype(jnp.int32), r1.astype(jnp.int32), zi], axis=0)
    zf = jnp.zeros((6, tm), F32)
    wgt_ref[...] = jnp.concatenate([p1 / den, p2 / den, zf], axis=0)


def _router(h, w_router, b_router):
    n, d = h.shape
    tm = ROW_TILE
    wr = _pad2(w_router, d, LANES)
    br = _pad2(b_router[None], 1, LANES)
    return pl.pallas_call(
        _router_kernel,
        grid=(n // tm,),
        in_specs=[pl.BlockSpec((tm, d), lambda i: (i, 0)),
                  pl.BlockSpec((d, LANES), lambda i: (0, 0)),
                  pl.BlockSpec((1, LANES), lambda i: (0, 0))],
        out_specs=[pl.BlockSpec((8, tm), lambda i: (0, i)),
                   pl.BlockSpec((8, tm), lambda i: (0, i)),
                   pl.BlockSpec((N_EXPERTS, LANES), lambda i: (0, 0))],
        out_shape=[jax.ShapeDtypeStruct((8, n), jnp.int32),
                   jax.ShapeDtypeStruct((8, n), F32),
                   jax.ShapeDtypeStruct((N_EXPERTS, LANES), F32)],
        scratch_shapes=[pltpu.VMEM((N_EXPERTS, LANES), F32)],
        compiler_params=_cparams(1, 32),
        name="router",
    )(h, wr, br)


def _expert_kernel(texp_ref, nv_ref, stok_ref, h_hbm, wg_ref, wu_ref, wd_ref, o_ref, xbuf, sem, *, tm):
    i = pl.program_id(0)
    nv = nv_ref[0]

    def issue(tile, slot):
        def body(r, carry):
            tok = stok_ref[tile * tm + r]
            pltpu.make_async_copy(h_hbm.at[pl.ds(tok, 1), :], xbuf.at[slot, pl.ds(r, 1), :],
                                  sem.at[slot]).start()
            return carry
        lax.fori_loop(0, tm, body, 0, unroll=GATHER_UNROLL)

    @pl.when(i == 0)
    def _():
        issue(0, 0)

    @pl.when(i + 1 < nv)
    def _():
        issue(i + 1, (i + 1) % 2)

    @pl.when(i < nv)
    def _():
        slot = i % 2
        pltpu.make_async_copy(h_hbm.at[pl.ds(0, tm), :], xbuf.at[slot], sem.at[slot]).wait()
        x = xbuf[slot].astype(BF16)
        g = jnp.dot(x, wg_ref[...], preferred_element_type=F32)
        u = jnp.dot(x, wu_ref[...], preferred_element_type=F32)
        hid = (g * _sigmoid(g) * u).astype(BF16)
        o_ref[...] = jnp.dot(hid, wd_ref[...], preferred_element_type=F32)

    @pl.when(i >= nv)
    def _():
        o_ref[...] = jnp.zeros_like(o_ref)


def _experts(h, texp, nvalid, slot_tok, wg, wu, wd, layer):
    n, d = h.shape
    de = wg.shape[-1]
    tm = EXPERT_TM
    n_tiles = slot_tok.shape[0] // tm
    grid_spec = pltpu.PrefetchScalarGridSpec(
        num_scalar_prefetch=3,
        grid=(n_tiles,),
        in_specs=[pl.BlockSpec(memory_space=pl.ANY),
                  pl.BlockSpec((None, None, d, de), lambda i, te, nv, st: (layer, te[i], 0, 0)),
                  pl.BlockSpec((None, None, d, de), lambda i, te, nv, st: (layer, te[i], 0, 0)),
                  pl.BlockSpec((None, None, de, d), lambda i, te, nv, st: (layer, te[i], 0, 0))],
        out_specs=pl.BlockSpec((tm, d), lambda i, te, nv, st: (i, 0)),
        scratch_shapes=[pltpu.VMEM((2, tm, d), F32), pltpu.SemaphoreType.DMA((2,))],
    )
    return pl.pallas_call(
        functools.partial(_expert_kernel, tm=tm),
        grid_spec=grid_spec,
        out_shape=jax.ShapeDtypeStruct((n_tiles * tm, d), F32),
        compiler_params=_cparams(1, 56, row_gather=True),
        name="experts",
    )(texp, nvalid, slot_tok, h, wg, wu, wd)


def _combine_kernel(p0_ref, p1_ref, ys_hbm, h_ref, w0_ref, w1_ref, g_ref, b_ref, o_ref, ob_ref,
                    gbuf, sem, *, tm, alpha):
    i = pl.program_id(0)
    nt = pl.num_programs(0)

    def issue(tile, slot):
        def body(r, carry):
            t = tile * tm + r
            pltpu.make_async_copy(ys_hbm.at[pl.ds(p0_ref[t], 1), :], gbuf.at[slot, 0, pl.ds(r, 1), :],
                                  sem.at[slot]).start()
            pltpu.make_async_copy(ys_hbm.at[pl.ds(p1_ref[t], 1), :], gbuf.at[slot, 1, pl.ds(r, 1), :],
                                  sem.at[slot]).start()
            return carry
        lax.fori_loop(0, tm, body, 0, unroll=GATHER_UNROLL)

    @pl.when(i == 0)
    def _():
        issue(0, 0)

    @pl.when(i + 1 < nt)
    def _():
        issue(i + 1, (i + 1) % 2)

    slot = i % 2
    for j in range(2):
        pltpu.make_async_copy(ys_hbm.at[pl.ds(0, tm), :], gbuf.at[slot, j], sem.at[slot]).wait()
    y = w0_ref[...] * gbuf[slot, 0] + w1_ref[...] * gbuf[slot, 1]
    out = _layer_norm(alpha * h_ref[...] + y, g_ref[...], b_ref[...])
    o_ref[...] = out
    ob_ref[...] = out.astype(BF16)


def _combine(ys, h, pos0, pos1, w0, w1, g2, b2, layer, alpha):
    n, d = h.shape
    tm = ROW_TILE
    row = pl.BlockSpec((tm, d), lambda i, a, b: (i, 0))
    col = pl.BlockSpec((tm, 1), lambda i, a, b: (i, 0))
    vec = pl.BlockSpec((None, 1, d), lambda i, a, b: (layer, 0, 0))
    grid_spec = pltpu.PrefetchScalarGridSpec(
        num_scalar_prefetch=2,
        grid=(n // tm,),
        in_specs=[pl.BlockSpec(memory_space=pl.ANY), row, col, col, vec, vec],
        out_specs=[row, row],
        scratch_shapes=[pltpu.VMEM((2, 2, tm, d), F32), pltpu.SemaphoreType.DMA((2,))],
    )
    return pl.pallas_call(
        functools.partial(_combine_kernel, tm=tm, alpha=alpha),
        grid_spec=grid_spec,
        out_shape=[jax.ShapeDtypeStruct((n, d), F32), jax.ShapeDtypeStruct((n, d), BF16)],
        compiler_params=_cparams(1, 40, row_gather=True),
        name="combine",
    )(pos0, pos1, ys, h, w0, w1, g2, b2)


def _routing_tables(idx, cnt, n):
    tm = EXPERT_TM
    e0, e1, r0, r1 = idx[0], idx[1], idx[2], idx[3]
    counts = cnt[:, 0].astype(jnp.int32)
    padded = ((counts + tm - 1) // tm) * tm
    pend = jnp.cumsum(padded)
    poff = pend - padded
    eid = jnp.arange(N_EXPERTS, dtype=jnp.int32)[:, None]
    pos0 = jnp.sum(jnp.where(e0[None, :] == eid, poff[:, None], 0), axis=0) + r0
    pos1 = jnp.sum(jnp.where(e1[None, :] == eid, poff[:, None], 0), axis=0) + r1
    n_slots = 2 * n + N_EXPERTS * tm
    tok = jnp.arange(n, dtype=jnp.int32)
    slot_tok = jnp.zeros((n_slots,), jnp.int32).at[jnp.concatenate([pos0, pos1])].set(
        jnp.concatenate([tok, tok]), unique_indices=True)
    n_tiles = n_slots // tm
    nvalid = pend[-1] // tm
    tile = jnp.arange(n_tiles, dtype=jnp.int32)
    start = jnp.minimum(tile, nvalid - 1) * tm
    texp = jnp.sum((start[:, None] >= pend[None, :]).astype(jnp.int32), axis=1)
    texp = jnp.minimum(texp, N_EXPERTS - 1)
    return pos0, pos1, slot_tok, texp, nvalid.reshape(1).astype(jnp.int32)


def kernel(x, in_ln_g, in_ln_b, w_in, b_in, attn_rpb, hy_conv_w, hy_conv_b, hy_f_w1, hy_f_b1, hy_f_w2, hy_f_b2, hy_f_w3, hy_f_b3, hy_f_freq, hy_f_w4, hy_skip, cf_dw_w, cf_dw_b, cf_ln_g, cf_ln_b, w_attn_br, w_hy_br, w_cf_br, w_o, b_o, ln1_g, ln1_b, w_router, b_router, moe_w_gate, moe_w_up, moe_w_down, ln2_g, ln2_b):
    bsz, l, d = x.shape
    depth = w_in.shape[0]
    n = bsz * l
    mw = w_attn_br.shape[1]
    alpha = (2 * depth) ** 0.25
    c_qkv, c_hy, c_cf, c_gate = 0, 3 * mw, 6 * mw, 8 * mw

    w_in_b = w_in.astype(BF16)
    b_in3 = b_in.astype(F32)[:, None, :]
    wa_b, wh_b, wc_b, wo_b = (w.astype(BF16) for w in (w_attn_br, w_hy_br, w_cf_br, w_o))
    wg_b, wu_b, wd_b = (w.astype(BF16) for w in (moe_w_gate, moe_w_up, moe_w_down))
    vec3 = lambda v: v.astype(F32)[:, None, :]
    bo3, cg3, cb3, g13, b13, g23, b23 = map(vec3, (b_o, cf_ln_g, cf_ln_b, ln1_g, ln1_b, ln2_g, ln2_b))
    fwd_np, inv_np = _dft_matrices(l, DFT_TK)
    fwd, inv = jnp.asarray(fwd_np), jnp.asarray(inv_np)

    h, hb = _in_ln(x.reshape(n, d), in_ln_g, in_ln_b)
    for layer in range(depth):
        proj = lambda c0, nc, dt, nm: _matmul(hb, w_in_b, b_in3, col0=c0, ncols=nc, out_dtype=dt,
                                              layer=layer, name=nm)
        qkv = proj(c_qkv, 3 * mw, BF16, "proj_qkv")
        hy_in = proj(c_hy, 3 * mw, F32, "proj_hyena")
        cf_in = proj(c_cf, 2 * mw, F32, "proj_conformer")
        gates = proj(c_gate, 3 * d, BF16, "proj_gates")

        y_a = _natten(qkv.reshape(bsz, l, 3 * mw), _natten_bias_table(attn_rpb[layer]))

        taps = _hy_filters(l, hy_f_w1[layer], hy_f_b1[layer], hy_f_w2[layer], hy_f_b2[layer],
                           hy_f_w3[layer], hy_f_b3[layer], hy_f_freq[layer], hy_f_w4[layer])
        spec = _matmul(fwd, taps, None, col0=0, ncols=taps.shape[1], out_dtype=F32, name="hy_spectrum")
        uc = _short_conv(hy_in.reshape(bsz, l, 3 * mw), hy_conv_w[layer], hy_conv_b[layer])
        z1 = _long_conv(uc, 0, uc, 1, hy_skip[layer, 0], spec, 0, fwd, inv)
        y_h = _long_conv(z1, 0, uc, 2, hy_skip[layer, 1], spec, 1, fwd, inv)

        y_c = _cf_conv(cf_in.reshape(bsz, l, 2 * mw), cf_dw_w[layer], cf_dw_b[layer])

        h = _merge(y_a.reshape(n, mw), y_h.reshape(n, mw), y_c.reshape(n, mw), gates, h,
                   wa_b, wh_b, wc_b, wo_b, bo3, cg3, cb3, g13, b13, layer, alpha)

        idx, wgt, cnt = _router(h, w_router, b_router)
        pos0, pos1, slot_tok, texp, nvalid = _routing_tables(idx, cnt, n)
        ys = _experts(h, texp, nvalid, slot_tok, wg_b, wu_b, wd_b, layer)
        h, hb = _combine(ys, h, pos0, pos1, wgt[0].reshape(n, 1), wgt[1].reshape(n, 1),
                         g23, b23, layer, alpha)
    return h.reshape(bsz, l, d)
```

```python
import functools
import math

import numpy as np
import jax
import jax.numpy as jnp
from jax import lax
from jax.experimental import pallas as pl
from jax.experimental.pallas import tpu as pltpu

F32 = jnp.float32
BF16 = jnp.bfloat16

GRID_W = 64
NA_HEAD_DIM = 64
NA_KH = 8
NA_KW = 16
HY_ORDER = 2
HY_POS_DIM = 33
HY_FAST_DECAY = 0.3
HY_SLOW_DECAY = 1.5
HY_DECAY_TARGET = 1e-2
CF_K = 31
N_GROUPS = 4
EXPERTS_PER_GROUP = 4
N_EXPERTS = N_GROUPS * EXPERTS_PER_GROUP
LN_EPS = 1e-5
NEG_INF = -1e30

LANES = 128
V7X_VMEM_BYTES = 64 * 1024 * 1024
MIB = 1024 * 1024

ROW_TILE = 256
MM_TM = 1024
MM_TN = 768
DFT_TK = 512
CONV_CB = 256
EXPERT_TM = 512
GATHER_UNROLL = 8
NATTEN_BATCH = 8


def _cparams(n_axes, vmem_mib, row_gather=False):
    assert vmem_mib * MIB < V7X_VMEM_BYTES
    return pltpu.CompilerParams(dimension_semantics=("arbitrary",) * n_axes,
                                vmem_limit_bytes=vmem_mib * MIB,
                                disable_bounds_checks=row_gather)


def _layer_norm(x, g, b):
    mu = jnp.mean(x, axis=-1, keepdims=True)
    xc = x - mu
    var = jnp.mean(xc * xc, axis=-1, keepdims=True)
    return xc * lax.rsqrt(var + LN_EPS) * g + b


def _sigmoid(x):
    return 1.0 / (1.0 + jnp.exp(-x))


def _in_ln_kernel(x_ref, g_ref, b_ref, h_ref, hb_ref):
    y = _layer_norm(x_ref[...], g_ref[...], b_ref[...])
    h_ref[...] = y
    hb_ref[...] = y.astype(BF16)


def _in_ln(x2, g, b):
    n, d = x2.shape
    row = pl.BlockSpec((ROW_TILE, d), lambda i: (i, 0))
    vec = pl.BlockSpec((1, d), lambda i: (0, 0))
    return pl.pallas_call(
        _in_ln_kernel,
        grid=(n // ROW_TILE,),
        in_specs=[row, vec, vec],
        out_specs=[row, row],
        out_shape=[jax.ShapeDtypeStruct((n, d), F32), jax.ShapeDtypeStruct((n, d), BF16)],
        compiler_params=_cparams(1, 32),
        name="in_ln",
    )(x2, g.reshape(1, d), b.reshape(1, d))


def _proj_kernel(x_ref, w_ref, b_ref, o_ref, wb_ref):
    @pl.when(pl.program_id(1) == 0)
    def _():
        wb_ref[...] = w_ref[...].astype(BF16)

    acc = jnp.dot(x_ref[...], wb_ref[...], preferred_element_type=F32)
    o_ref[...] = (acc + b_ref[...]).astype(o_ref.dtype)


def _proj(x, w, bias, *, layer, col0, ncols, out_dtype, name):
    m, k = x.shape
    tm = min(MM_TM, m)
    tn = MM_TN
    assert m % tm == 0 and ncols % tn == 0 and col0 % tn == 0
    jb = col0 // tn
    return pl.pallas_call(
        _proj_kernel,
        grid=(ncols // tn, m // tm),
        in_specs=[pl.BlockSpec((tm, k), lambda j, i: (i, 0)),
                  pl.BlockSpec((None, k, tn), lambda j, i: (layer, 0, j + jb)),
                  pl.BlockSpec((None, 1, tn), lambda j, i: (layer, 0, j + jb))],
        out_specs=pl.BlockSpec((tm, tn), lambda j, i: (i, j)),
        out_shape=jax.ShapeDtypeStruct((m, ncols), out_dtype),
        scratch_shapes=[pltpu.VMEM((k, tn), BF16)],
        compiler_params=_cparams(2, 48),
        name=name,
    )(x, w, bias)


def _natten_bias_table(rpb):
    qc = np.arange(GRID_W)
    kc = np.arange(GRID_W)
    cs = np.clip(qc - NA_KW // 2, 0, GRID_W - NA_KW)
    valid = (kc[None, :] >= cs[:, None]) & (kc[None, :] < cs[:, None] + NA_KW)
    col_off = np.clip(kc[None, :] - qc[:, None], -(NA_KW - 1), NA_KW - 1) + (NA_KW - 1)
    row_off = np.arange(NA_KH)[None, :] - np.arange(NA_KH)[:, None] + (NA_KH - 1)
    row_sel = (row_off[:, :, None] == np.arange(2 * NA_KH - 1)).astype(np.float32)
    col_sel = (col_off[:, :, None] == np.arange(2 * NA_KW - 1)).astype(np.float32)
    b = jnp.einsum("hrc,djr,qkc->dhqjk", rpb.astype(F32), row_sel, col_sel,
                   precision=lax.Precision.HIGHEST)
    b = jnp.where(jnp.asarray(valid)[None, None, :, None, :], b, NEG_INF)
    h = rpb.shape[0]
    return b.reshape(NA_KH, h, GRID_W, NA_KH * GRID_W)


def _natten_kernel(q_ref, k_ref, v_ref, bias_ref, o_ref, s_ref, p_ref, *, rows, batch):
    win = NA_KH * GRID_W
    pair_rows = 2 * GRID_W
    lane = lax.broadcasted_iota(jnp.int32, (GRID_W, 2 * NA_HEAD_DIM), 1)
    first = lane < NA_HEAD_DIM
    scale = jnp.asarray(NA_HEAD_DIM ** -0.5, BF16)
    window_start = lambda r: min(max(r - NA_KH // 2, 0), rows - NA_KH)
    for r0 in range(0, rows, batch):
        for i in range(batch):
            r = r0 + i
            rs = window_start(r)
            q = q_ref[0, r * GRID_W:(r + 1) * GRID_W, :] * scale
            zero = jnp.zeros_like(q)
            qm = jnp.concatenate([jnp.where(first, q, zero), jnp.where(first, zero, q)], axis=0)
            kw = k_ref[0, rs * GRID_W:rs * GRID_W + win, :]
            s = lax.dot_general(qm, kw, (((1,), (1,)), ((), ())), preferred_element_type=F32)
            s_ref[i * pair_rows:(i + 1) * pair_rows, :] = s + bias_ref[r - rs].reshape(pair_rows, win)
        s = s_ref[...]
        e = jnp.exp(s - jnp.max(s, axis=-1, keepdims=True))
        p_ref[...] = (e / jnp.sum(e, axis=-1, keepdims=True)).astype(BF16)
        for i in range(batch):
            r = r0 + i
            rs = window_start(r)
            vw = v_ref[0, rs * GRID_W:rs * GRID_W + win, :]
            o = jnp.dot(p_ref[i * pair_rows:(i + 1) * pair_rows, :], vw, preferred_element_type=F32)
            o_ref[0, r * GRID_W:(r + 1) * GRID_W, :] = (
                jnp.where(first, o[:GRID_W], o[GRID_W:]).astype(o_ref.dtype))


def _natten(qkv, bias_tab):
    b, l, w3 = qkv.shape
    w = w3 // 3
    pair = 2 * NA_HEAD_DIM
    npairs = w // pair
    rows = l // GRID_W
    batch = NATTEN_BATCH
    assert rows >= NA_KH and rows % batch == 0
    blk = lambda off: pl.BlockSpec((1, l, pair), lambda bi, hp: (bi, 0, hp + off))
    return pl.pallas_call(
        functools.partial(_natten_kernel, rows=rows, batch=batch),
        grid=(b, npairs),
        in_specs=[blk(0), blk(npairs), blk(2 * npairs),
                  pl.BlockSpec((NA_KH, 2, GRID_W, NA_KH * GRID_W), lambda bi, hp: (0, hp, 0, 0))],
        out_specs=pl.BlockSpec((1, l, pair), lambda bi, hp: (bi, 0, hp)),
        out_shape=jax.ShapeDtypeStruct((b, l, w), BF16),
        scratch_shapes=[pltpu.VMEM((batch * 2 * GRID_W, NA_KH * GRID_W), F32),
                        pltpu.VMEM((batch * 2 * GRID_W, NA_KH * GRID_W), BF16)],
        compiler_params=_cparams(2, 32),
        name="natten",
    )(qkv, qkv, qkv, bias_tab)


@functools.lru_cache(maxsize=None)
def _dft_matrices(l, tk):
    n2 = 2 * l
    k = np.arange(l, dtype=np.int64)
    n = np.arange(l, dtype=np.int64)
    ang = 2.0 * np.pi * ((k[:, None] * n[None, :]) % n2).astype(np.float64) / n2
    f_re = np.cos(ang)
    f_im = -np.sin(ang)
    f_im[0, :] = np.cos(np.pi * n)
    g_re = (2.0 / n2) * np.cos(ang).T
    g_re[:, 0] = 1.0 / n2
    g_im = -(2.0 / n2) * np.sin(ang).T
    g_im[:, 0] = np.cos(np.pi * n) / n2
    kt = l // tk
    fwd = np.stack([f_re.reshape(kt, tk, l), f_im.reshape(kt, tk, l)], axis=1).reshape(2 * l, l)
    inv = np.stack([g_re.reshape(l, kt, tk), g_im.reshape(l, kt, tk)], axis=2).reshape(l, 2 * l)
    return np.asarray(fwd, dtype=BF16), np.asarray(inv, dtype=BF16)


def _hy_filter_kernel(z_ref, t_ref, dl_ref, w1_ref, b1_ref, w2_ref, b2_ref, w3_ref, b3_ref,
                      fr_ref, w4_ref, o_ref, *, half):
    hp = lax.Precision.HIGHEST
    fr = fr_ref[...]
    h = jnp.sin(fr * (jnp.dot(z_ref[...], w1_ref[...], precision=hp, preferred_element_type=F32) + b1_ref[...]))
    h = jnp.sin(fr * (jnp.dot(h, w2_ref[...], precision=hp, preferred_element_type=F32) + b2_ref[...]))
    h = jnp.sin(fr * (jnp.dot(h, w3_ref[...], precision=hp, preferred_element_type=F32) + b3_ref[...]))
    h = jnp.dot(h, w4_ref[...], precision=hp, preferred_element_type=F32)
    h = h * jnp.exp(-t_ref[...] * dl_ref[...])
    tl, nc = h.shape
    row = lax.broadcasted_iota(jnp.int32, (tl, nc), 0) + pl.program_id(0) * tl
    col = lax.broadcasted_iota(jnp.int32, (tl, nc), 1)
    h = jnp.where((row == 0) & (col >= half), 0.0, h)
    o_ref[...] = h.astype(o_ref.dtype)


def _pad2(a, r, c):
    return jnp.pad(a.astype(F32), ((0, r - a.shape[0]), (0, c - a.shape[1])))


def _hy_filters(l, w1, b1, w2, b2, w3, b3, freq, w4):
    t = jnp.linspace(0.0, 1.0, l, dtype=F32)[:, None]
    bands = (HY_POS_DIM - 1) // 2
    w = 2.0 * math.pi * jnp.arange(l, dtype=F32)[:, None] / l
    f = jnp.linspace(1e-4, bands - 1, bands, dtype=F32)[None, :]
    z = jnp.concatenate([t, jnp.cos(f * w), -jnp.sin(f * w)], axis=-1)
    nc = w4.shape[1]
    max_decay = math.log(HY_DECAY_TARGET) / HY_FAST_DECAY
    min_decay = math.log(HY_DECAY_TARGET) / HY_SLOW_DECAY
    deltas = jnp.abs(jnp.linspace(min_decay, max_decay, nc, dtype=F32))[None, :]
    hid = LANES
    tl = min(256, l)
    full = lambda r, c: pl.BlockSpec((r, c), lambda i: (0, 0))
    return pl.pallas_call(
        functools.partial(_hy_filter_kernel, half=nc // 2),
        grid=(l // tl,),
        in_specs=[pl.BlockSpec((tl, hid), lambda i: (i, 0)), pl.BlockSpec((tl, 1), lambda i: (i, 0)),
                  full(1, nc), full(hid, hid), full(1, hid), full(hid, hid), full(1, hid),
                  full(hid, hid), full(1, hid), full(1, hid), full(hid, nc)],
        out_specs=pl.BlockSpec((tl, nc), lambda i: (i, 0)),
        out_shape=jax.ShapeDtypeStruct((l, nc), BF16),
        compiler_params=_cparams(1, 32),
        name="hy_filter",
    )(_pad2(z, l, hid), t, deltas, _pad2(w1, hid, hid), _pad2(b1[None], 1, hid),
      _pad2(w2, hid, hid), _pad2(b2[None], 1, hid), _pad2(w3, hid, hid), _pad2(b3[None], 1, hid),
      _pad2(freq[None], 1, hid), _pad2(w4, hid, nc))


def _spectrum_kernel(f_ref, tf_ref, tb_ref, o_ref, *, tk):
    sf = jnp.dot(f_ref[...], tf_ref[...], preferred_element_type=F32)
    sb = jnp.dot(f_ref[...], tb_ref[...], preferred_element_type=F32)
    im = sf[tk:] - sb[tk:]
    nyq = (lax.broadcasted_iota(jnp.int32, im.shape, 0) == 0) & (pl.program_id(0) == 0)
    o_ref[0:tk, :] = sf[:tk] + sb[:tk]
    o_ref[tk:2 * tk, :] = jnp.where(nyq, sf[tk:] + sb[tk:], im)


def _hy_spectrum(fwd, taps, tk):
    l2, l = fwd.shape
    c = taps.shape[1] // (2 * HY_ORDER)
    return pl.pallas_call(
        functools.partial(_spectrum_kernel, tk=tk),
        grid=(l2 // (2 * tk), HY_ORDER),
        in_specs=[pl.BlockSpec((2 * tk, l), lambda ki, o: (ki, 0)),
                  pl.BlockSpec((l, c), lambda ki, o: (0, o)),
                  pl.BlockSpec((l, c), lambda ki, o: (0, HY_ORDER + o))],
        out_specs=pl.BlockSpec((2 * tk, c), lambda ki, o: (ki, o)),
        out_shape=jax.ShapeDtypeStruct((l2, HY_ORDER * c), F32),
        compiler_params=_cparams(2, 40),
        name="hy_spectrum",
    )(fwd, taps, taps)


def _short_conv3(u_ref, w_ref, b_ref, o_ref):
    l, c = o_ref.shape
    row = lax.broadcasted_iota(jnp.int32, (l, CONV_CB), 0)
    for c0 in range(0, c, CONV_CB):
        u = u_ref[0, :, c0:c0 + CONV_CB].astype(F32)
        prev = jnp.where(row == 0, 0.0, pltpu.roll(u, 1, 0))
        nxt = jnp.where(row == l - 1, 0.0, pltpu.roll(u, l - 1, 0))
        w = w_ref[:, c0:c0 + CONV_CB]
        out = w[0:1] * prev + w[1:2] * u + w[2:3] * nxt + b_ref[:, c0:c0 + CONV_CB]
        o_ref[:, c0:c0 + CONV_CB] = out.astype(o_ref.dtype)


def _long_conv_kernel(x_ref, gate_ref, xw_ref, xb_ref, gw_ref, gb_ref, skip_ref, f_ref, g_ref, h_ref,
                      o_ref, acc_ref, xs_ref, gs_ref, *, tk, conv_x):
    kt = pl.program_id(1)

    @pl.when(kt == 0)
    def _():
        acc_ref[...] = jnp.zeros_like(acc_ref)
        if conv_x:
            _short_conv3(x_ref, xw_ref, xb_ref, xs_ref)
        else:
            xs_ref[...] = x_ref[0]
        _short_conv3(gate_ref, gw_ref, gb_ref, gs_ref)

    z = jnp.dot(f_ref[...], xs_ref[...], preferred_element_type=F32)
    zr, zi = z[:tk], z[tk:]
    hr, hi = h_ref[0:tk, :], h_ref[tk:2 * tk, :]
    nyq = (lax.broadcasted_iota(jnp.int32, zr.shape, 0) == 0) & (kt == 0)
    yr = zr * hr - jnp.where(nyq, 0.0, zi * hi)
    yi = jnp.where(nyq, zi * hi, zr * hi + zi * hr)
    y = jnp.concatenate([yr, yi], axis=0).astype(BF16)
    acc_ref[...] += jnp.dot(g_ref[...], y, preferred_element_type=F32)

    @pl.when(kt == pl.num_programs(1) - 1)
    def _():
        o_ref[0] = (gs_ref[...].astype(F32) * (acc_ref[...] + xs_ref[...].astype(F32) * skip_ref[...])
                    ).astype(o_ref.dtype)


def _long_conv(x_arr, x_blk, conv_x, u, gate_blk, conv_w, conv_b, skip, spec, order, fwd, inv, tk):
    b, l, _ = x_arr.shape
    c = skip.shape[-1]
    kt = l // tk
    once = pl.Buffered(1)
    xcol = x_blk if conv_x else 0
    return pl.pallas_call(
        functools.partial(_long_conv_kernel, tk=tk, conv_x=conv_x),
        grid=(b, kt),
        in_specs=[pl.BlockSpec((1, l, c), lambda bi, ki: (bi, 0, x_blk), pipeline_mode=once),
                  pl.BlockSpec((1, l, c), lambda bi, ki: (bi, 0, gate_blk), pipeline_mode=once),
                  pl.BlockSpec((3, c), lambda bi, ki: (0, xcol)),
                  pl.BlockSpec((1, c), lambda bi, ki: (0, xcol)),
                  pl.BlockSpec((3, c), lambda bi, ki: (0, gate_blk)),
                  pl.BlockSpec((1, c), lambda bi, ki: (0, gate_blk)),
                  pl.BlockSpec((1, c), lambda bi, ki: (0, 0)),
                  pl.BlockSpec((2 * tk, l), lambda bi, ki: (ki, 0)),
                  pl.BlockSpec((l, 2 * tk), lambda bi, ki: (0, ki)),
                  pl.BlockSpec((2 * tk, c), lambda bi, ki: (ki, order))],
        out_specs=pl.BlockSpec((1, l, c), lambda bi, ki: (bi, 0, 0)),
        out_shape=jax.ShapeDtypeStruct((b, l, c), BF16),
        scratch_shapes=[pltpu.VMEM((l, c), F32), pltpu.VMEM((l, c), BF16), pltpu.VMEM((l, c), BF16)],
        compiler_params=_cparams(2, 56),
        name=f"hy_long_conv{order}",
    )(x_arr, u, conv_w, conv_b, conv_w, conv_b, skip.astype(F32).reshape(1, c), fwd, inv, spec)


def _cf_conv_kernel(a_ref, g_ref, w_ref, b_ref, o_ref, zpad_ref, zsh_ref, *, chunk):
    l = a_ref.shape[1]
    pad = 16
    sub = 8
    z = a_ref[0] * _sigmoid(g_ref[0])
    zeros = jnp.zeros((pad, z.shape[1]), F32)
    zpad_ref[0:pad, :] = zeros
    zpad_ref[pad:pad + l, :] = z
    zpad_ref[pad + l:pad + l + pad, :] = zeros
    half = CF_K // 2
    nsh = zsh_ref.shape[0]
    for s in range(sub):
        zsh_ref[...] = zpad_ref[s:s + nsh, :]
        taps = [k for k in range(CF_K) if (pad - half + k) % sub == s]
        for c in range(l // chunk):
            if s == 0:
                acc = jnp.broadcast_to(b_ref[...], (chunk, z.shape[1]))
            else:
                acc = o_ref[0, c * chunk:(c + 1) * chunk, :]
            for k in taps:
                r0 = c * chunk + (pad - half + k) - s
                acc = acc + w_ref[k:k + 1, :] * zsh_ref[r0:r0 + chunk, :]
            o_ref[0, c * chunk:(c + 1) * chunk, :] = acc


def _cf_conv(cf_in, w, bias):
    b, l, w2 = cf_in.shape
    wd = w2 // 2
    cb = CONV_CB
    nb = wd // cb
    chunk = min(256, l)
    return pl.pallas_call(
        functools.partial(_cf_conv_kernel, chunk=chunk),
        grid=(b, nb),
        in_specs=[pl.BlockSpec((1, l, cb), lambda bi, ci: (bi, 0, ci)),
                  pl.BlockSpec((1, l, cb), lambda bi, ci: (bi, 0, ci + nb)),
                  pl.BlockSpec((CF_K, cb), lambda bi, ci: (0, ci)),
                  pl.BlockSpec((1, cb), lambda bi, ci: (0, ci))],
        out_specs=pl.BlockSpec((1, l, cb), lambda bi, ci: (bi, 0, ci)),
        out_shape=jax.ShapeDtypeStruct((b, l, wd), F32),
        scratch_shapes=[pltpu.VMEM((l + 32, cb), F32), pltpu.VMEM((l + 24, cb), F32)],
        compiler_params=_cparams(2, 32),
        name="cf_conv",
    )(cf_in, cf_in, w.astype(F32), bias.astype(F32).reshape(1, wd))


def _merge_kernel(ya_ref, yh_ref, yc_ref, ga_ref, gh_ref, gc_ref, h_ref, wa_ref, wh_ref, wc_ref,
                  wo_ref, bo_ref, cg_ref, cb_ref, g1_ref, b1_ref, o_ref, *, alpha):
    yc = _layer_norm(yc_ref[...], cg_ref[...], cb_ref[...])
    yc = (yc * _sigmoid(yc)).astype(BF16)
    m = _sigmoid(ga_ref[...].astype(F32)) * jnp.dot(ya_ref[...], wa_ref[...], preferred_element_type=F32)
    m = m + _sigmoid(gh_ref[...].astype(F32)) * jnp.dot(yh_ref[...], wh_ref[...], preferred_element_type=F32)
    m = m + _sigmoid(gc_ref[...].astype(F32)) * jnp.dot(yc, wc_ref[...], preferred_element_type=F32)
    mix = jnp.dot(m.astype(BF16), wo_ref[...], preferred_element_type=F32) + bo_ref[...]
    o_ref[...] = _layer_norm(alpha * h_ref[...] + mix, g1_ref[...], b1_ref[...])


def _merge(ya, yh, yc, gates, h, wa, wh, wc, wo, bo, cg, cb, g1, b1, layer, alpha):
    n, d = h.shape
    w = ya.shape[1]
    tm = ROW_TILE
    row = lambda c: pl.BlockSpec((tm, c), lambda i: (i, 0))
    gate = lambda j: pl.BlockSpec((tm, d), lambda i: (i, j))
    once = pl.Buffered(1)
    wspec = lambda r, c: pl.BlockSpec((None, r, c), lambda i: (layer, 0, 0), pipeline_mode=once)
    vec = lambda c: pl.BlockSpec((None, 1, c), lambda i: (layer, 0, 0))
    return pl.pallas_call(
        functools.partial(_merge_kernel, alpha=alpha),
        grid=(n // tm,),
        in_specs=[row(w), row(w), row(w), gate(0), gate(1), gate(2), row(d),
                  wspec(w, d), wspec(w, d), wspec(w, d), wspec(d, d),
                  vec(d), vec(w), vec(w), vec(d), vec(d)],
        out_specs=row(d),
        out_shape=jax.ShapeDtypeStruct((n, d), F32),
        compiler_params=_cparams(1, 48),
        name="merge",
    )(ya, yh, yc, gates, gates, gates, h, wa, wh, wc, wo, bo, cg, cb, g1, b1)


def _first_max(vals):
    m = vals[0]
    for v in vals[1:]:
        m = jnp.maximum(m, v)
    idx = jnp.full(m.shape, len(vals) - 1, jnp.int32)
    for j in range(len(vals) - 2, -1, -1):
        idx = jnp.where(vals[j] == m, j, idx)
    return m, idx


def _top2(vals):
    m1, i1 = _first_max(vals)
    rest = [jnp.where(i1 == j, -1.0, v) for j, v in enumerate(vals)]
    m2, i2 = _first_max(rest)
    return m1, i1, m2, i2


def _router_kernel(h_ref, wr_ref, br_ref, idx_ref, wgt_ref, cnt_ref, carry_ref):
    i = pl.program_id(0)

    @pl.when(i == 0)
    def _():
        carry_ref[...] = jnp.zeros_like(carry_ref)

    h = h_ref[...]
    h_hi = h.astype(BF16)
    h_lo = (h - h_hi.astype(F32)).astype(BF16)
    w = wr_ref[...]
    w_hi = w.astype(BF16)
    w_lo = (w - w_hi.astype(F32)).astype(BF16)
    logits = (jnp.dot(h_hi, w_hi, preferred_element_type=F32) + jnp.dot(h_hi, w_lo, preferred_element_type=F32)
              + jnp.dot(h_lo, w_hi, preferred_element_type=F32)) + br_ref[...]
    lt = logits.T[:N_EXPERTS]
    tm = lt.shape[1]
    mx = jnp.max(lt, axis=0, keepdims=True)
    ex = jnp.exp(lt - mx)
    probs = ex / jnp.sum(ex, axis=0, keepdims=True)
    p = [probs[e:e + 1, :] for e in range(N_EXPERTS)]
    scores = []
    for g in range(N_GROUPS):
        a, _, b, _ = _top2(p[g * EXPERTS_PER_GROUP:(g + 1) * EXPERTS_PER_GROUP])
        scores.append(a + b)
    _, g_sel = _first_max(scores)
    pg = []
    for j in range(EXPERTS_PER_GROUP):
        v = p[(N_GROUPS - 1) * EXPERTS_PER_GROUP + j]
        for g in range(N_GROUPS - 2, -1, -1):
            v = jnp.where(g_sel == g, p[g * EXPERTS_PER_GROUP + j], v)
        pg.append(v)
    p1, i1, p2, i2 = _top2(pg)
    den = p1 + p2
    e0 = g_sel * EXPERTS_PER_GROUP + i1
    e1 = g_sel * EXPERTS_PER_GROUP + i2

    erow = lax.broadcasted_iota(jnp.int32, (N_EXPERTS, tm), 0)
    oh0 = (erow == e0).astype(F32)
    oh1 = (erow == e1).astype(F32)
    both = oh0 + oh1
    before = (lax.broadcasted_iota(jnp.int32, (tm, tm), 0) < lax.broadcasted_iota(jnp.int32, (tm, tm), 1))
    cum = jnp.dot(both.astype(BF16), before.astype(BF16), preferred_element_type=F32) + carry_ref[:, 0:1]
    r0 = jnp.sum(oh0 * cum, axis=0, keepdims=True)
    r1 = jnp.sum(oh1 * cum, axis=0, keepdims=True)
    carry_ref[...] = carry_ref[...] + jnp.sum(both, axis=1, keepdims=True)
    cnt_ref[...] = carry_ref[...]

    zi = jnp.zeros((4, tm), jnp.int32)
    idx_ref[...] = jnp.concatenate([e0, e1, r0.astype(jnp.int32), r1.astype(jnp.int32), zi], axis=0)
    zf = jnp.zeros((6, tm), F32)
    wgt_ref[...] = jnp.concatenate([p1 / den, p2 / den, zf], axis=0)


def _router(h, w_router, b_router):
    n, d = h.shape
    tm = ROW_TILE
    wr = _pad2(w_router, d, LANES)
    br = _pad2(b_router[None], 1, LANES)
    return pl.pallas_call(
        _router_kernel,
        grid=(n // tm,),
        in_specs=[pl.BlockSpec((tm, d), lambda i: (i, 0)),
                  pl.BlockSpec((d, LANES), lambda i: (0, 0)),
                  pl.BlockSpec((1, LANES), lambda i: (0, 0))],
        out_specs=[pl.BlockSpec((8, tm), lambda i: (0, i)),
                   pl.BlockSpec((8, tm), lambda i: (0, i)),
                   pl.BlockSpec((N_EXPERTS, LANES), lambda i: (0, 0))],
        out_shape=[jax.ShapeDtypeStruct((8, n), jnp.int32),
                   jax.ShapeDtypeStruct((8, n), F32),
                   jax.ShapeDtypeStruct((N_EXPERTS, LANES), F32)],
        scratch_shapes=[pltpu.VMEM((N_EXPERTS, LANES), F32)],
        compiler_params=_cparams(1, 32),
        name="router",
    )(h, wr, br)


def _expert_kernel(texp_ref, nv_ref, stok_ref, h_hbm, wg_ref, wu_ref, wd_ref, o_ref, xbuf, sem, *, tm):
    i = pl.program_id(0)
    nv = nv_ref[0]

    def row_copy(tok, slot, r):
        return pltpu.make_async_copy(h_hbm.at[pl.ds(tok, 1), :], xbuf.at[slot, pl.ds(r, 1), :], sem.at[slot])

    def wait_tile(slot):
        pltpu.make_async_copy(h_hbm.at[pl.ds(0, tm), :], xbuf.at[slot], sem.at[slot]).wait()

    @pl.when(i == 0)
    def _():
        def body(r, carry):
            row_copy(stok_ref[r], 0, r).start()
            return carry
        lax.fori_loop(0, tm, body, 0, unroll=GATHER_UNROLL)

    @pl.when(i < nv)
    def _():
        slot = i % 2
        wait_tile(slot)
        x = xbuf[slot].astype(BF16)
        base = jnp.minimum(i + 1, nv - 1) * tm
        for r in range(tm):
            row_copy(stok_ref[base + r], 1 - slot, r).start()
        g = jnp.dot(x, wg_ref[...], preferred_element_type=F32)
        u = jnp.dot(x, wu_ref[...], preferred_element_type=F32)
        hid = (g * _sigmoid(g) * u).astype(BF16)
        o_ref[...] = jnp.dot(hid, wd_ref[...], preferred_element_type=F32)

    @pl.when(i == nv - 1)
    def _():
        wait_tile(1 - i % 2)

    @pl.when(i >= nv)
    def _():
        o_ref[...] = jnp.zeros_like(o_ref)


def _experts(h, texp, nvalid, slot_tok, wg, wu, wd, layer):
    n, d = h.shape
    de = wg.shape[-1]
    tm = EXPERT_TM
    n_tiles = slot_tok.shape[0] // tm
    grid_spec = pltpu.PrefetchScalarGridSpec(
        num_scalar_prefetch=3,
        grid=(n_tiles,),
        in_specs=[pl.BlockSpec(memory_space=pl.ANY),
                  pl.BlockSpec((None, None, d, de), lambda i, te, nv, st: (layer, te[i], 0, 0)),
                  pl.BlockSpec((None, None, d, de), lambda i, te, nv, st: (layer, te[i], 0, 0)),
                  pl.BlockSpec((None, None, de, d), lambda i, te, nv, st: (layer, te[i], 0, 0))],
        out_specs=pl.BlockSpec((tm, d), lambda i, te, nv, st: (i, 0)),
        scratch_shapes=[pltpu.VMEM((2, tm, d), F32), pltpu.SemaphoreType.DMA((2,))],
    )
    return pl.pallas_call(
        functools.partial(_expert_kernel, tm=tm),
        grid_spec=grid_spec,
        out_shape=jax.ShapeDtypeStruct((n_tiles * tm, d), F32),
        compiler_params=_cparams(1, 56, row_gather=True),
        name="experts",
    )(texp, nvalid, slot_tok, h, wg, wu, wd)


def _combine_kernel(p0_ref, p1_ref, ys_hbm, h_ref, w0_ref, w1_ref, g_ref, b_ref, o_ref, ob_ref,
                    gbuf, sem, *, tm, alpha):
    i = pl.program_id(0)
    nt = pl.num_programs(0)

    def start_rows(t, slot, r):
        pltpu.make_async_copy(ys_hbm.at[pl.ds(p0_ref[t], 1), :], gbuf.at[slot, 0, pl.ds(r, 1), :],
                              sem.at[slot]).start()
        pltpu.make_async_copy(ys_hbm.at[pl.ds(p1_ref[t], 1), :], gbuf.at[slot, 1, pl.ds(r, 1), :],
                              sem.at[slot]).start()

    def wait_tile(slot):
        for j in range(2):
            pltpu.make_async_copy(ys_hbm.at[pl.ds(0, tm), :], gbuf.at[slot, j], sem.at[slot]).wait()

    @pl.when(i == 0)
    def _():
        def body(r, carry):
            start_rows(r, 0, r)
            return carry
        lax.fori_loop(0, tm, body, 0, unroll=GATHER_UNROLL)

    slot = i % 2
    wait_tile(slot)
    base = jnp.minimum(i + 1, nt - 1) * tm
    for r in range(tm):
        start_rows(base + r, 1 - slot, r)
    y = w0_ref[...] * gbuf[slot, 0] + w1_ref[...] * gbuf[slot, 1]
    out = _layer_norm(alpha * h_ref[...] + y, g_ref[...], b_ref[...])
    o_ref[...] = out
    ob_ref[...] = out.astype(BF16)

    @pl.when(i == nt - 1)
    def _():
        wait_tile(1 - slot)


def _combine(ys, h, pos0, pos1, w0, w1, g2, b2, layer, alpha):
    n, d = h.shape
    tm = ROW_TILE
    row = pl.BlockSpec((tm, d), lambda i, a, b: (i, 0))
    col = pl.BlockSpec((tm, 1), lambda i, a, b: (i, 0))
    vec = pl.BlockSpec((None, 1, d), lambda i, a, b: (layer, 0, 0))
    grid_spec = pltpu.PrefetchScalarGridSpec(
        num_scalar_prefetch=2,
        grid=(n // tm,),
        in_specs=[pl.BlockSpec(memory_space=pl.ANY), row, col, col, vec, vec],
        out_specs=[row, row],
        scratch_shapes=[pltpu.VMEM((2, 2, tm, d), F32), pltpu.SemaphoreType.DMA((2,))],
    )
    return pl.pallas_call(
        functools.partial(_combine_kernel, tm=tm, alpha=alpha),
        grid_spec=grid_spec,
        out_shape=[jax.ShapeDtypeStruct((n, d), F32), jax.ShapeDtypeStruct((n, d), BF16)],
        compiler_params=_cparams(1, 40, row_gather=True),
        name="combine",
    )(pos0, pos1, ys, h, w0, w1, g2, b2)


def _routing_tables(idx, cnt, n):
    tm = EXPERT_TM
    e0, e1, r0, r1 = idx[0], idx[1], idx[2], idx[3]
    counts = cnt[:, 0].astype(jnp.int32)
    padded = ((counts + tm - 1) // tm) * tm
    pend = jnp.cumsum(padded)
    poff = pend - padded
    eid = jnp.arange(N_EXPERTS, dtype=jnp.int32)[:, None]
    pos0 = jnp.sum(jnp.where(e0[None, :] == eid, poff[:, None], 0), axis=0) + r0
    pos1 = jnp.sum(jnp.where(e1[None, :] == eid, poff[:, None], 0), axis=0) + r1
    n_slots = 2 * n + N_EXPERTS * tm
    tok = jnp.arange(n, dtype=jnp.int32)
    slot_tok = jnp.zeros((n_slots,), jnp.int32).at[jnp.concatenate([pos0, pos1])].set(
        jnp.concatenate([tok, tok]), unique_indices=True)
    n_tiles = n_slots // tm
    nvalid = pend[-1] // tm
    tile = jnp.arange(n_tiles, dtype=jnp.int32)
    start = jnp.minimum(tile, nvalid - 1) * tm
    texp = jnp.sum((start[:, None] >= pend[None, :]).astype(jnp.int32), axis=1)
    texp = jnp.minimum(texp, N_EXPERTS - 1)
    return pos0, pos1, slot_tok, texp, nvalid.reshape(1).astype(jnp.int32)


def kernel(x, in_ln_g, in_ln_b, w_in, b_in, attn_rpb, hy_conv_w, hy_conv_b, hy_f_w1, hy_f_b1, hy_f_w2, hy_f_b2, hy_f_w3, hy_f_b3, hy_f_freq, hy_f_w4, hy_skip, cf_dw_w, cf_dw_b, cf_ln_g, cf_ln_b, w_attn_br, w_hy_br, w_cf_br, w_o, b_o, ln1_g, ln1_b, w_router, b_router, moe_w_gate, moe_w_up, moe_w_down, ln2_g, ln2_b):
    bsz, l, d = x.shape
    depth = w_in.shape[0]
    n = bsz * l
    mw = w_attn_br.shape[1]
    alpha = (2 * depth) ** 0.25
    c_qkv, c_hy, c_cf, c_gate = 0, 3 * mw, 6 * mw, 8 * mw

    w_in_f = w_in.astype(F32)
    b_in3 = b_in.astype(F32)[:, None, :]
    wa_b, wh_b, wc_b, wo_b = (w.astype(BF16) for w in (w_attn_br, w_hy_br, w_cf_br, w_o))
    wg_b, wu_b, wd_b = (w.astype(BF16) for w in (moe_w_gate, moe_w_up, moe_w_down))
    vec3 = lambda v: v.astype(F32)[:, None, :]
    bo3, cg3, cb3, g13, b13, g23, b23 = map(vec3, (b_o, cf_ln_g, cf_ln_b, ln1_g, ln1_b, ln2_g, ln2_b))
    tk = min(DFT_TK, l)
    fwd_np, inv_np = _dft_matrices(l, tk)
    fwd, inv = jnp.asarray(fwd_np), jnp.asarray(inv_np)

    h, hb = _in_ln(x.reshape(n, d), in_ln_g, in_ln_b)
    for layer in range(depth):
        proj = lambda c0, nc, dt, nm: _proj(hb, w_in_f, b_in3, layer=layer, col0=c0, ncols=nc,
                                            out_dtype=dt, name=nm)
        qkv = proj(c_qkv, 3 * mw, BF16, "proj_qkv")
        hy_in = proj(c_hy, 3 * mw, BF16, "proj_hyena")
        cf_in = proj(c_cf, 2 * mw, F32, "proj_conformer")
        gates = proj(c_gate, 3 * d, BF16, "proj_gates")

        y_a = _natten(qkv.reshape(bsz, l, 3 * mw), _natten_bias_table(attn_rpb[layer]))

        taps = _hy_filters(l, hy_f_w1[layer], hy_f_b1[layer], hy_f_w2[layer], hy_f_b2[layer],
                           hy_f_w3[layer], hy_f_b3[layer], hy_f_freq[layer], hy_f_w4[layer])
        spec = _hy_spectrum(fwd, taps, tk)
        u = hy_in.reshape(bsz, l, 3 * mw)
        cw, cb_ = hy_conv_w[layer].astype(F32), hy_conv_b[layer].astype(F32)[None]
        z1 = _long_conv(u, 0, True, u, 1, cw, cb_, hy_skip[layer, 0], spec, 0, fwd, inv, tk)
        y_h = _long_conv(z1, 0, False, u, 2, cw, cb_, hy_skip[layer, 1], spec, 1, fwd, inv, tk)

        y_c = _cf_conv(cf_in.reshape(bsz, l, 2 * mw), cf_dw_w[layer], cf_dw_b[layer])

        h = _merge(y_a.reshape(n, mw), y_h.reshape(n, mw), y_c.reshape(n, mw), gates, h,
                   wa_b, wh_b, wc_b, wo_b, bo3, cg3, cb3, g13, b13, layer, alpha)

        idx, wgt, cnt = _router(h, w_router, b_router)
        pos0, pos1, slot_tok, texp, nvalid = _routing_tables(idx, cnt, n)
        ys = _experts(h, texp, nvalid, slot_tok, wg_b, wu_b, wd_b, layer)
        h, hb = _combine(ys, h, pos0, pos1, wgt[0].reshape(n, 1), wgt[1].reshape(n, 1),
                         g23, b23, layer, alpha)
    return h.reshape(bsz, l, d)
```

```python
import functools
import math

import numpy as np
import jax
import jax.numpy as jnp
from jax import lax
from jax.experimental import pallas as pl
from jax.experimental.pallas import tpu as pltpu

F32 = jnp.float32
BF16 = jnp.bfloat16

GRID_W = 64
NA_HEAD_DIM = 64
NA_KH = 8
NA_KW = 16
HY_ORDER = 2
HY_POS_DIM = 33
HY_FAST_DECAY = 0.3
HY_SLOW_DECAY = 1.5
HY_DECAY_TARGET = 1e-2
CF_K = 31
N_GROUPS = 4
EXPERTS_PER_GROUP = 4
N_EXPERTS = N_GROUPS * EXPERTS_PER_GROUP
LN_EPS = 1e-5
NEG_INF = -1e30

LANES = 128
V7X_VMEM_BYTES = 64 * 1024 * 1024
MIB = 1024 * 1024

ROW_TILE = 256
MM_TM = 1024
MM_TN = 768
DFT_TK = 512
CONV_CB = 256
EXPERT_TM = 256
GATHER_UNROLL = 8
NATTEN_BATCH = 8


def _cparams(n_axes, vmem_mib, row_gather=False):
    assert vmem_mib * MIB < V7X_VMEM_BYTES
    return pltpu.CompilerParams(dimension_semantics=("arbitrary",) * n_axes,
                                vmem_limit_bytes=vmem_mib * MIB,
                                disable_bounds_checks=row_gather)


def _layer_norm(x, g, b):
    mu = jnp.mean(x, axis=-1, keepdims=True)
    xc = x - mu
    var = jnp.mean(xc * xc, axis=-1, keepdims=True)
    return xc * lax.rsqrt(var + LN_EPS) * g + b


def _sigmoid(x):
    return 1.0 / (1.0 + jnp.exp(-x))


def _in_ln_kernel(x_ref, g_ref, b_ref, h_ref, hb_ref):
    y = _layer_norm(x_ref[...], g_ref[...], b_ref[...])
    h_ref[...] = y
    hb_ref[...] = y.astype(BF16)


def _in_ln(x2, g, b):
    n, d = x2.shape
    row = pl.BlockSpec((ROW_TILE, d), lambda i: (i, 0))
    vec = pl.BlockSpec((1, d), lambda i: (0, 0))
    return pl.pallas_call(
        _in_ln_kernel,
        grid=(n // ROW_TILE,),
        in_specs=[row, vec, vec],
        out_specs=[row, row],
        out_shape=[jax.ShapeDtypeStruct((n, d), F32), jax.ShapeDtypeStruct((n, d), BF16)],
        compiler_params=_cparams(1, 32),
        name="in_ln",
    )(x2, g.reshape(1, d), b.reshape(1, d))


def _proj_kernel(x_ref, w_ref, b_ref, o_ref, wb_ref):
    @pl.when(pl.program_id(1) == 0)
    def _():
        wb_ref[...] = w_ref[...].astype(BF16)

    acc = jnp.dot(x_ref[...], wb_ref[...], preferred_element_type=F32)
    o_ref[...] = (acc + b_ref[...]).astype(o_ref.dtype)


def _proj(x, w, bias, *, layer, col0, ncols, out_dtype, name):
    m, k = x.shape
    tm = min(MM_TM, m)
    tn = MM_TN
    assert m % tm == 0 and ncols % tn == 0 and col0 % tn == 0
    jb = col0 // tn
    return pl.pallas_call(
        _proj_kernel,
        grid=(ncols // tn, m // tm),
        in_specs=[pl.BlockSpec((tm, k), lambda j, i: (i, 0)),
                  pl.BlockSpec((None, k, tn), lambda j, i: (layer, 0, j + jb)),
                  pl.BlockSpec((None, 1, tn), lambda j, i: (layer, 0, j + jb))],
        out_specs=pl.BlockSpec((tm, tn), lambda j, i: (i, j)),
        out_shape=jax.ShapeDtypeStruct((m, ncols), out_dtype),
        scratch_shapes=[pltpu.VMEM((k, tn), BF16)],
        compiler_params=_cparams(2, 48),
        name=name,
    )(x, w, bias)


def _natten_bias_table(rpb):
    qc = np.arange(GRID_W)
    kc = np.arange(GRID_W)
    cs = np.clip(qc - NA_KW // 2, 0, GRID_W - NA_KW)
    valid = (kc[None, :] >= cs[:, None]) & (kc[None, :] < cs[:, None] + NA_KW)
    col_off = np.clip(kc[None, :] - qc[:, None], -(NA_KW - 1), NA_KW - 1) + (NA_KW - 1)
    row_off = np.arange(NA_KH)[None, :] - np.arange(NA_KH)[:, None] + (NA_KH - 1)
    row_sel = (row_off[:, :, None] == np.arange(2 * NA_KH - 1)).astype(np.float32)
    col_sel = (col_off[:, :, None] == np.arange(2 * NA_KW - 1)).astype(np.float32)
    b = jnp.einsum("hrc,djr,qkc->dhqjk", rpb.astype(F32), row_sel, col_sel,
                   precision=lax.Precision.HIGHEST)
    b = jnp.where(jnp.asarray(valid)[None, None, :, None, :], b, NEG_INF)
    h = rpb.shape[0]
    return b.reshape(NA_KH, h, GRID_W, NA_KH * GRID_W)


def _natten_kernel(q_ref, k_ref, v_ref, bias_ref, o_ref, s_ref, p_ref, *, rows, batch):
    win = NA_KH * GRID_W
    pair_rows = 2 * GRID_W
    lane = lax.broadcasted_iota(jnp.int32, (GRID_W, 2 * NA_HEAD_DIM), 1)
    first = lane < NA_HEAD_DIM
    scale = jnp.asarray(NA_HEAD_DIM ** -0.5, BF16)
    window_start = lambda r: min(max(r - NA_KH // 2, 0), rows - NA_KH)
    for r0 in range(0, rows, batch):
        for i in range(batch):
            r = r0 + i
            rs = window_start(r)
            q = q_ref[0, r * GRID_W:(r + 1) * GRID_W, :] * scale
            zero = jnp.zeros_like(q)
            qm = jnp.concatenate([jnp.where(first, q, zero), jnp.where(first, zero, q)], axis=0)
            kw = k_ref[0, rs * GRID_W:rs * GRID_W + win, :]
            s = lax.dot_general(qm, kw, (((1,), (1,)), ((), ())), preferred_element_type=F32)
            s_ref[i * pair_rows:(i + 1) * pair_rows, :] = s + bias_ref[r - rs].reshape(pair_rows, win)
        s = s_ref[...]
        e = jnp.exp(s - jnp.max(s, axis=-1, keepdims=True))
        p_ref[...] = (e / jnp.sum(e, axis=-1, keepdims=True)).astype(BF16)
        for i in range(batch):
            r = r0 + i
            rs = window_start(r)
            vw = v_ref[0, rs * GRID_W:rs * GRID_W + win, :]
            o = jnp.dot(p_ref[i * pair_rows:(i + 1) * pair_rows, :], vw, preferred_element_type=F32)
            o_ref[0, r * GRID_W:(r + 1) * GRID_W, :] = (
                jnp.where(first, o[:GRID_W], o[GRID_W:]).astype(o_ref.dtype))


def _natten(qkv, bias_tab):
    b, l, w3 = qkv.shape
    w = w3 // 3
    pair = 2 * NA_HEAD_DIM
    npairs = w // pair
    rows = l // GRID_W
    batch = NATTEN_BATCH
    assert rows >= NA_KH and rows % batch == 0
    blk = lambda off: pl.BlockSpec((1, l, pair), lambda bi, hp: (bi, 0, hp + off))
    return pl.pallas_call(
        functools.partial(_natten_kernel, rows=rows, batch=batch),
        grid=(b, npairs),
        in_specs=[blk(0), blk(npairs), blk(2 * npairs),
                  pl.BlockSpec((NA_KH, 2, GRID_W, NA_KH * GRID_W), lambda bi, hp: (0, hp, 0, 0))],
        out_specs=pl.BlockSpec((1, l, pair), lambda bi, hp: (bi, 0, hp)),
        out_shape=jax.ShapeDtypeStruct((b, l, w), BF16),
        scratch_shapes=[pltpu.VMEM((batch * 2 * GRID_W, NA_KH * GRID_W), F32),
                        pltpu.VMEM((batch * 2 * GRID_W, NA_KH * GRID_W), BF16)],
        compiler_params=_cparams(2, 32),
        name="natten",
    )(qkv, qkv, qkv, bias_tab)


@functools.lru_cache(maxsize=None)
def _dft_matrices(l, tk):
    n2 = 2 * l
    k = np.arange(l, dtype=np.int64)
    n = np.arange(l, dtype=np.int64)
    ang = 2.0 * np.pi * ((k[:, None] * n[None, :]) % n2).astype(np.float64) / n2
    f_re = np.cos(ang)
    f_im = -np.sin(ang)
    f_im[0, :] = np.cos(np.pi * n)
    g_re = (2.0 / n2) * np.cos(ang).T
    g_re[:, 0] = 1.0 / n2
    g_im = -(2.0 / n2) * np.sin(ang).T
    g_im[:, 0] = np.cos(np.pi * n) / n2
    kt = l // tk
    fwd = np.stack([f_re.reshape(kt, tk, l), f_im.reshape(kt, tk, l)], axis=1).reshape(2 * l, l)
    inv = np.stack([g_re.reshape(l, kt, tk), g_im.reshape(l, kt, tk)], axis=2).reshape(l, 2 * l)
    return np.asarray(fwd, dtype=BF16), np.asarray(inv, dtype=BF16)


def _hy_filter_kernel(z_ref, t_ref, dl_ref, w1_ref, b1_ref, w2_ref, b2_ref, w3_ref, b3_ref,
                      fr_ref, w4_ref, o_ref, *, half):
    hp = lax.Precision.HIGHEST
    fr = fr_ref[...]
    h = jnp.sin(fr * (jnp.dot(z_ref[...], w1_ref[...], precision=hp, preferred_element_type=F32) + b1_ref[...]))
    h = jnp.sin(fr * (jnp.dot(h, w2_ref[...], precision=hp, preferred_element_type=F32) + b2_ref[...]))
    h = jnp.sin(fr * (jnp.dot(h, w3_ref[...], precision=hp, preferred_element_type=F32) + b3_ref[...]))
    h = jnp.dot(h, w4_ref[...], precision=hp, preferred_element_type=F32)
    h = h * jnp.exp(-t_ref[...] * dl_ref[...])
    tl, nc = h.shape
    row = lax.broadcasted_iota(jnp.int32, (tl, nc), 0) + pl.program_id(0) * tl
    col = lax.broadcasted_iota(jnp.int32, (tl, nc), 1)
    h = jnp.where((row == 0) & (col >= half), 0.0, h)
    o_ref[...] = h.astype(o_ref.dtype)


def _pad2(a, r, c):
    return jnp.pad(a.astype(F32), ((0, r - a.shape[0]), (0, c - a.shape[1])))


def _hy_filters(l, w1, b1, w2, b2, w3, b3, freq, w4):
    t = jnp.linspace(0.0, 1.0, l, dtype=F32)[:, None]
    bands = (HY_POS_DIM - 1) // 2
    w = 2.0 * math.pi * jnp.arange(l, dtype=F32)[:, None] / l
    f = jnp.linspace(1e-4, bands - 1, bands, dtype=F32)[None, :]
    z = jnp.concatenate([t, jnp.cos(f * w), -jnp.sin(f * w)], axis=-1)
    nc = w4.shape[1]
    max_decay = math.log(HY_DECAY_TARGET) / HY_FAST_DECAY
    min_decay = math.log(HY_DECAY_TARGET) / HY_SLOW_DECAY
    deltas = jnp.abs(jnp.linspace(min_decay, max_decay, nc, dtype=F32))[None, :]
    hid = LANES
    tl = min(256, l)
    full = lambda r, c: pl.BlockSpec((r, c), lambda i: (0, 0))
    return pl.pallas_call(
        functools.partial(_hy_filter_kernel, half=nc // 2),
        grid=(l // tl,),
        in_specs=[pl.BlockSpec((tl, hid), lambda i: (i, 0)), pl.BlockSpec((tl, 1), lambda i: (i, 0)),
                  full(1, nc), full(hid, hid), full(1, hid), full(hid, hid), full(1, hid),
                  full(hid, hid), full(1, hid), full(1, hid), full(hid, nc)],
        out_specs=pl.BlockSpec((tl, nc), lambda i: (i, 0)),
        out_shape=jax.ShapeDtypeStruct((l, nc), BF16),
        compiler_params=_cparams(1, 32),
        name="hy_filter",
    )(_pad2(z, l, hid), t, deltas, _pad2(w1, hid, hid), _pad2(b1[None], 1, hid),
      _pad2(w2, hid, hid), _pad2(b2[None], 1, hid), _pad2(w3, hid, hid), _pad2(b3[None], 1, hid),
      _pad2(freq[None], 1, hid), _pad2(w4, hid, nc))


def _spectrum_kernel(f_ref, tf_ref, tb_ref, o_ref, *, tk):
    sf = jnp.dot(f_ref[...], tf_ref[...], preferred_element_type=F32)
    sb = jnp.dot(f_ref[...], tb_ref[...], preferred_element_type=F32)
    im = sf[tk:] - sb[tk:]
    nyq = (lax.broadcasted_iota(jnp.int32, im.shape, 0) == 0) & (pl.program_id(0) == 0)
    o_ref[0:tk, :] = sf[:tk] + sb[:tk]
    o_ref[tk:2 * tk, :] = jnp.where(nyq, sf[tk:] + sb[tk:], im)


def _hy_spectrum(fwd, taps, tk):
    l2, l = fwd.shape
    c = taps.shape[1] // (2 * HY_ORDER)
    return pl.pallas_call(
        functools.partial(_spectrum_kernel, tk=tk),
        grid=(l2 // (2 * tk), HY_ORDER),
        in_specs=[pl.BlockSpec((2 * tk, l), lambda ki, o: (ki, 0)),
                  pl.BlockSpec((l, c), lambda ki, o: (0, o)),
                  pl.BlockSpec((l, c), lambda ki, o: (0, HY_ORDER + o))],
        out_specs=pl.BlockSpec((2 * tk, c), lambda ki, o: (ki, o)),
        out_shape=jax.ShapeDtypeStruct((l2, HY_ORDER * c), F32),
        compiler_params=_cparams(2, 40),
        name="hy_spectrum",
    )(fwd, taps, taps)


def _short_conv3(u_ref, w_ref, b_ref, o_ref):
    l, c = o_ref.shape
    row = lax.broadcasted_iota(jnp.int32, (l, CONV_CB), 0)
    for c0 in range(0, c, CONV_CB):
        u = u_ref[0, :, c0:c0 + CONV_CB].astype(F32)
        prev = jnp.where(row == 0, 0.0, pltpu.roll(u, 1, 0))
        nxt = jnp.where(row == l - 1, 0.0, pltpu.roll(u, l - 1, 0))
        w = w_ref[:, c0:c0 + CONV_CB]
        out = w[0:1] * prev + w[1:2] * u + w[2:3] * nxt + b_ref[:, c0:c0 + CONV_CB]
        o_ref[:, c0:c0 + CONV_CB] = out.astype(o_ref.dtype)


def _long_conv_kernel(x_ref, gate_ref, xw_ref, xb_ref, gw_ref, gb_ref, skip_ref, f_ref, g_ref, h_ref,
                      o_ref, acc_ref, xs_ref, gs_ref, *, tk, conv_x):
    kt = pl.program_id(1)

    @pl.when(kt == 0)
    def _():
        acc_ref[...] = jnp.zeros_like(acc_ref)
        if conv_x:
            _short_conv3(x_ref, xw_ref, xb_ref, xs_ref)
        else:
            xs_ref[...] = x_ref[0]
        _short_conv3(gate_ref, gw_ref, gb_ref, gs_ref)

    z = jnp.dot(f_ref[...], xs_ref[...], preferred_element_type=F32)
    zr, zi = z[:tk], z[tk:]
    hr, hi = h_ref[0:tk, :], h_ref[tk:2 * tk, :]
    nyq = (lax.broadcasted_iota(jnp.int32, zr.shape, 0) == 0) & (kt == 0)
    yr = zr * hr - jnp.where(nyq, 0.0, zi * hi)
    yi = jnp.where(nyq, zi * hi, zr * hi + zi * hr)
    y = jnp.concatenate([yr, yi], axis=0).astype(BF16)
    acc_ref[...] += jnp.dot(g_ref[...], y, preferred_element_type=F32)

    @pl.when(kt == pl.num_programs(1) - 1)
    def _():
        o_ref[0] = (gs_ref[...].astype(F32) * (acc_ref[...] + xs_ref[...].astype(F32) * skip_ref[...])
                    ).astype(o_ref.dtype)


def _long_conv(x_arr, x_blk, conv_x, u, gate_blk, conv_w, conv_b, skip, spec, order, fwd, inv, tk):
    b, l, _ = x_arr.shape
    c = skip.shape[-1]
    kt = l // tk
    once = pl.Buffered(1)
    xcol = x_blk if conv_x else 0
    return pl.pallas_call(
        functools.partial(_long_conv_kernel, tk=tk, conv_x=conv_x),
        grid=(b, kt),
        in_specs=[pl.BlockSpec((1, l, c), lambda bi, ki: (bi, 0, x_blk), pipeline_mode=once),
                  pl.BlockSpec((1, l, c), lambda bi, ki: (bi, 0, gate_blk), pipeline_mode=once),
                  pl.BlockSpec((3, c), lambda bi, ki: (0, xcol)),
                  pl.BlockSpec((1, c), lambda bi, ki: (0, xcol)),
                  pl.BlockSpec((3, c), lambda bi, ki: (0, gate_blk)),
                  pl.BlockSpec((1, c), lambda bi, ki: (0, gate_blk)),
                  pl.BlockSpec((1, c), lambda bi, ki: (0, 0)),
                  pl.BlockSpec((2 * tk, l), lambda bi, ki: (ki, 0)),
                  pl.BlockSpec((l, 2 * tk), lambda bi, ki: (0, ki)),
                  pl.BlockSpec((2 * tk, c), lambda bi, ki: (ki, order))],
        out_specs=pl.BlockSpec((1, l, c), lambda bi, ki: (bi, 0, 0)),
        out_shape=jax.ShapeDtypeStruct((b, l, c), BF16),
        scratch_shapes=[pltpu.VMEM((l, c), F32), pltpu.VMEM((l, c), BF16), pltpu.VMEM((l, c), BF16)],
        compiler_params=_cparams(2, 56),
        name=f"hy_long_conv{order}",
    )(x_arr, u, conv_w, conv_b, conv_w, conv_b, skip.astype(F32).reshape(1, c), fwd, inv, spec)


def _cf_conv_kernel(a_ref, g_ref, w_ref, b_ref, o_ref, zpad_ref, zsh_ref, *, chunk):
    l = a_ref.shape[1]
    pad = 16
    sub = 8
    z = a_ref[0] * _sigmoid(g_ref[0])
    zeros = jnp.zeros((pad, z.shape[1]), F32)
    zpad_ref[0:pad, :] = zeros
    zpad_ref[pad:pad + l, :] = z
    zpad_ref[pad + l:pad + l + pad, :] = zeros
    half = CF_K // 2
    nsh = zsh_ref.shape[0]
    for s in range(sub):
        zsh_ref[...] = zpad_ref[s:s + nsh, :]
        taps = [k for k in range(CF_K) if (pad - half + k) % sub == s]
        for c in range(l // chunk):
            if s == 0:
                acc = jnp.broadcast_to(b_ref[...], (chunk, z.shape[1]))
            else:
                acc = o_ref[0, c * chunk:(c + 1) * chunk, :]
            for k in taps:
                r0 = c * chunk + (pad - half + k) - s
                acc = acc + w_ref[k:k + 1, :] * zsh_ref[r0:r0 + chunk, :]
            o_ref[0, c * chunk:(c + 1) * chunk, :] = acc


def _cf_conv(cf_in, w, bias):
    b, l, w2 = cf_in.shape
    wd = w2 // 2
    cb = CONV_CB
    nb = wd // cb
    chunk = min(256, l)
    return pl.pallas_call(
        functools.partial(_cf_conv_kernel, chunk=chunk),
        grid=(b, nb),
        in_specs=[pl.BlockSpec((1, l, cb), lambda bi, ci: (bi, 0, ci)),
                  pl.BlockSpec((1, l, cb), lambda bi, ci: (bi, 0, ci + nb)),
                  pl.BlockSpec((CF_K, cb), lambda bi, ci: (0, ci)),
                  pl.BlockSpec((1, cb), lambda bi, ci: (0, ci))],
        out_specs=pl.BlockSpec((1, l, cb), lambda bi, ci: (bi, 0, ci)),
        out_shape=jax.ShapeDtypeStruct((b, l, wd), F32),
        scratch_shapes=[pltpu.VMEM((l + 32, cb), F32), pltpu.VMEM((l + 24, cb), F32)],
        compiler_params=_cparams(2, 32),
        name="cf_conv",
    )(cf_in, cf_in, w.astype(F32), bias.astype(F32).reshape(1, wd))


def _merge_kernel(ya_ref, yh_ref, yc_ref, ga_ref, gh_ref, gc_ref, h_ref, wa_ref, wh_ref, wc_ref,
                  wo_ref, bo_ref, cg_ref, cb_ref, g1_ref, b1_ref, o_ref, *, alpha):
    yc = _layer_norm(yc_ref[...], cg_ref[...], cb_ref[...])
    yc = (yc * _sigmoid(yc)).astype(BF16)
    m = _sigmoid(ga_ref[...].astype(F32)) * jnp.dot(ya_ref[...], wa_ref[...], preferred_element_type=F32)
    m = m + _sigmoid(gh_ref[...].astype(F32)) * jnp.dot(yh_ref[...], wh_ref[...], preferred_element_type=F32)
    m = m + _sigmoid(gc_ref[...].astype(F32)) * jnp.dot(yc, wc_ref[...], preferred_element_type=F32)
    mix = jnp.dot(m.astype(BF16), wo_ref[...], preferred_element_type=F32) + bo_ref[...]
    o_ref[...] = _layer_norm(alpha * h_ref[...] + mix, g1_ref[...], b1_ref[...])


def _merge(ya, yh, yc, gates, h, wa, wh, wc, wo, bo, cg, cb, g1, b1, layer, alpha):
    n, d = h.shape
    w = ya.shape[1]
    tm = ROW_TILE
    row = lambda c: pl.BlockSpec((tm, c), lambda i: (i, 0))
    gate = lambda j: pl.BlockSpec((tm, d), lambda i: (i, j))
    once = pl.Buffered(1)
    wspec = lambda r, c: pl.BlockSpec((None, r, c), lambda i: (layer, 0, 0), pipeline_mode=once)
    vec = lambda c: pl.BlockSpec((None, 1, c), lambda i: (layer, 0, 0))
    return pl.pallas_call(
        functools.partial(_merge_kernel, alpha=alpha),
        grid=(n // tm,),
        in_specs=[row(w), row(w), row(w), gate(0), gate(1), gate(2), row(d),
                  wspec(w, d), wspec(w, d), wspec(w, d), wspec(d, d),
                  vec(d), vec(w), vec(w), vec(d), vec(d)],
        out_specs=row(d),
        out_shape=jax.ShapeDtypeStruct((n, d), F32),
        compiler_params=_cparams(1, 48),
        name="merge",
    )(ya, yh, yc, gates, gates, gates, h, wa, wh, wc, wo, bo, cg, cb, g1, b1)


def _first_max(vals):
    m = vals[0]
    for v in vals[1:]:
        m = jnp.maximum(m, v)
    idx = jnp.full(m.shape, len(vals) - 1, jnp.int32)
    for j in range(len(vals) - 2, -1, -1):
        idx = jnp.where(vals[j] == m, j, idx)
    return m, idx


def _top2(vals):
    m1, i1 = _first_max(vals)
    rest = [jnp.where(i1 == j, -1.0, v) for j, v in enumerate(vals)]
    m2, i2 = _first_max(rest)
    return m1, i1, m2, i2


def _router_kernel(h_ref, wr_ref, br_ref, idx_ref, wgt_ref, cnt_ref, carry_ref):
    i = pl.program_id(0)

    @pl.when(i == 0)
    def _():
        carry_ref[...] = jnp.zeros_like(carry_ref)

    h = h_ref[...]
    h_hi = h.astype(BF16)
    h_lo = (h - h_hi.astype(F32)).astype(BF16)
    w = wr_ref[...]
    w_hi = w.astype(BF16)
    w_lo = (w - w_hi.astype(F32)).astype(BF16)
    logits = (jnp.dot(h_hi, w_hi, preferred_element_type=F32) + jnp.dot(h_hi, w_lo, preferred_element_type=F32)
              + jnp.dot(h_lo, w_hi, preferred_element_type=F32)) + br_ref[...]
    lt = logits.T[:N_EXPERTS]
    tm = lt.shape[1]
    mx = jnp.max(lt, axis=0, keepdims=True)
    ex = jnp.exp(lt - mx)
    probs = ex / jnp.sum(ex, axis=0, keepdims=True)
    p = [probs[e:e + 1, :] for e in range(N_EXPERTS)]
    scores = []
    for g in range(N_GROUPS):
        a, _, b, _ = _top2(p[g * EXPERTS_PER_GROUP:(g + 1) * EXPERTS_PER_GROUP])
        scores.append(a + b)
    _, g_sel = _first_max(scores)
    pg = []
    for j in range(EXPERTS_PER_GROUP):
        v = p[(N_GROUPS - 1) * EXPERTS_PER_GROUP + j]
        for g in range(N_GROUPS - 2, -1, -1):
            v = jnp.where(g_sel == g, p[g * EXPERTS_PER_GROUP + j], v)
        pg.append(v)
    p1, i1, p2, i2 = _top2(pg)
    den = p1 + p2
    e0 = g_sel * EXPERTS_PER_GROUP + i1
    e1 = g_sel * EXPERTS_PER_GROUP + i2

    erow = lax.broadcasted_iota(jnp.int32, (N_EXPERTS, tm), 0)
    oh0 = (erow == e0).astype(F32)
    oh1 = (erow == e1).astype(F32)
    both = oh0 + oh1
    before = (lax.broadcasted_iota(jnp.int32, (tm, tm), 0) < lax.broadcasted_iota(jnp.int32, (tm, tm), 1))
    cum = jnp.dot(both.astype(BF16), before.astype(BF16), preferred_element_type=F32) + carry_ref[:, 0:1]
    r0 = jnp.sum(oh0 * cum, axis=0, keepdims=True)
    r1 = jnp.sum(oh1 * cum, axis=0, keepdims=True)
    carry_ref[...] = carry_ref[...] + jnp.sum(both, axis=1, keepdims=True)
    cnt_ref[...] = carry_ref[...]

    zi = jnp.zeros((4, tm), jnp.int32)
    idx_ref[...] = jnp.concatenate([e0, e1, r0.astype(jnp.int32), r1.astype(jnp.int32), zi], axis=0)
    zf = jnp.zeros((6, tm), F32)
    wgt_ref[...] = jnp.concatenate([p1 / den, p2 / den, zf], axis=0)


def _router(h, w_router, b_router):
    n, d = h.shape
    tm = ROW_TILE
    wr = _pad2(w_router, d, LANES)
    br = _pad2(b_router[None], 1, LANES)
    return pl.pallas_call(
        _router_kernel,
        grid=(n // tm,),
        in_specs=[pl.BlockSpec((tm, d), lambda i: (i, 0)),
                  pl.BlockSpec((d, LANES), lambda i: (0, 0)),
                  pl.BlockSpec((1, LANES), lambda i: (0, 0))],
        out_specs=[pl.BlockSpec((8, tm), lambda i: (0, i)),
                   pl.BlockSpec((8, tm), lambda i: (0, i)),
                   pl.BlockSpec((N_EXPERTS, LANES), lambda i: (0, 0))],
        out_shape=[jax.ShapeDtypeStruct((8, n), jnp.int32),
                   jax.ShapeDtypeStruct((8, n), F32),
                   jax.ShapeDtypeStruct((N_EXPERTS, LANES), F32)],
        scratch_shapes=[pltpu.VMEM((N_EXPERTS, LANES), F32)],
        compiler_params=_cparams(1, 32),
        name="router",
    )(h, wr, br)


def _expert_kernel(texp_ref, nv_ref, stok_ref, h_hbm, wg_ref, wu_ref, wd_ref, o_ref, xbuf, sem, *, tm):
    i = pl.program_id(0)
    nv = nv_ref[0]
    nbuf = xbuf.shape[0]

    def row_copy(tok, slot, r):
        return pltpu.make_async_copy(h_hbm.at[pl.ds(tok, 1), :], xbuf.at[slot, pl.ds(r, 1), :], sem.at[slot])

    def wait_tile(slot):
        pltpu.make_async_copy(h_hbm.at[pl.ds(0, tm), :], xbuf.at[slot], sem.at[slot]).wait()

    @pl.when(i == 0)
    def _():
        second = jnp.minimum(1, nv - 1) * tm

        def body(r, carry):
            row_copy(stok_ref[r], 0, r).start()
            row_copy(stok_ref[second + r], 1, r).start()
            return carry
        lax.fori_loop(0, tm, body, 0, unroll=GATHER_UNROLL)

    @pl.when(i < nv)
    def _():
        slot = i % nbuf
        wait_tile(slot)
        x = xbuf[slot].astype(BF16)
        g = jnp.dot(x, wg_ref[...], preferred_element_type=F32)
        u = jnp.dot(x, wu_ref[...], preferred_element_type=F32)
        hid = (g * _sigmoid(g) * u).astype(BF16)
        o_ref[...] = jnp.dot(hid, wd_ref[...], preferred_element_type=F32)
        base = jnp.minimum(i + 2, nv - 1) * tm
        nxt = (i + 2) % nbuf
        for r in range(tm):
            row_copy(stok_ref[base + r], nxt, r).start()

    @pl.when(i == nv - 1)
    def _():
        wait_tile(nv % nbuf)
        wait_tile((nv + 1) % nbuf)

    @pl.when(i >= nv)
    def _():
        o_ref[...] = jnp.zeros_like(o_ref)


def _experts(h, texp, nvalid, slot_tok, wg, wu, wd, layer):
    n, d = h.shape
    de = wg.shape[-1]
    tm = EXPERT_TM
    n_tiles = slot_tok.shape[0] // tm
    grid_spec = pltpu.PrefetchScalarGridSpec(
        num_scalar_prefetch=3,
        grid=(n_tiles,),
        in_specs=[pl.BlockSpec(memory_space=pl.ANY),
                  pl.BlockSpec((None, None, d, de), lambda i, te, nv, st: (layer, te[i], 0, 0)),
                  pl.BlockSpec((None, None, d, de), lambda i, te, nv, st: (layer, te[i], 0, 0)),
                  pl.BlockSpec((None, None, de, d), lambda i, te, nv, st: (layer, te[i], 0, 0))],
        out_specs=pl.BlockSpec((tm, d), lambda i, te, nv, st: (i, 0)),
        scratch_shapes=[pltpu.VMEM((3, tm, d), F32), pltpu.SemaphoreType.DMA((3,))],
    )
    return pl.pallas_call(
        functools.partial(_expert_kernel, tm=tm),
        grid_spec=grid_spec,
        out_shape=jax.ShapeDtypeStruct((n_tiles * tm, d), F32),
        compiler_params=_cparams(1, 48, row_gather=True),
        name="experts",
    )(texp, nvalid, slot_tok, h, wg, wu, wd)


def _combine_kernel(p0_ref, p1_ref, ys_hbm, h_ref, w0_ref, w1_ref, g_ref, b_ref, o_ref, ob_ref,
                    gbuf, sem, *, tm, alpha):
    i = pl.program_id(0)
    nt = pl.num_programs(0)

    def start_rows(t, slot, r):
        pltpu.make_async_copy(ys_hbm.at[pl.ds(p0_ref[t], 1), :], gbuf.at[slot, 0, pl.ds(r, 1), :],
                              sem.at[slot]).start()
        pltpu.make_async_copy(ys_hbm.at[pl.ds(p1_ref[t], 1), :], gbuf.at[slot, 1, pl.ds(r, 1), :],
                              sem.at[slot]).start()

    def wait_tile(slot):
        for j in range(2):
            pltpu.make_async_copy(ys_hbm.at[pl.ds(0, tm), :], gbuf.at[slot, j], sem.at[slot]).wait()

    @pl.when(i == 0)
    def _():
        def body(r, carry):
            start_rows(r, 0, r)
            return carry
        lax.fori_loop(0, tm, body, 0, unroll=GATHER_UNROLL)

    slot = i % 2
    wait_tile(slot)
    base = jnp.minimum(i + 1, nt - 1) * tm
    for r in range(tm):
        start_rows(base + r, 1 - slot, r)
    y = w0_ref[...] * gbuf[slot, 0] + w1_ref[...] * gbuf[slot, 1]
    out = _layer_norm(alpha * h_ref[...] + y, g_ref[...], b_ref[...])
    o_ref[...] = out
    ob_ref[...] = out.astype(BF16)

    @pl.when(i == nt - 1)
    def _():
        wait_tile(1 - slot)


def _combine(ys, h, pos0, pos1, w0, w1, g2, b2, layer, alpha):
    n, d = h.shape
    tm = ROW_TILE
    row = pl.BlockSpec((tm, d), lambda i, a, b: (i, 0))
    col = pl.BlockSpec((tm, 1), lambda i, a, b: (i, 0))
    vec = pl.BlockSpec((None, 1, d), lambda i, a, b: (layer, 0, 0))
    grid_spec = pltpu.PrefetchScalarGridSpec(
        num_scalar_prefetch=2,
        grid=(n // tm,),
        in_specs=[pl.BlockSpec(memory_space=pl.ANY), row, col, col, vec, vec],
        out_specs=[row, row],
        scratch_shapes=[pltpu.VMEM((2, 2, tm, d), F32), pltpu.SemaphoreType.DMA((2,))],
    )
    return pl.pallas_call(
        functools.partial(_combine_kernel, tm=tm, alpha=alpha),
        grid_spec=grid_spec,
        out_shape=[jax.ShapeDtypeStruct((n, d), F32), jax.ShapeDtypeStruct((n, d), BF16)],
        compiler_params=_cparams(1, 40, row_gather=True),
        name="combine",
    )(pos0, pos1, ys, h, w0, w1, g2, b2)


def _slot_table_kernel(p0_ref, p1_ref, o_ref):
    n_slots = o_ref.shape[0]
    n = p0_ref.shape[0]

    def zero(i, carry):
        o_ref[i] = 0
        return carry
    lax.fori_loop(0, n_slots, zero, 0, unroll=GATHER_UNROLL)

    def body(t, carry):
        o_ref[p0_ref[t]] = t
        o_ref[p1_ref[t]] = t
        return carry
    lax.fori_loop(0, n, body, 0, unroll=GATHER_UNROLL)


def _slot_table(pos0, pos1, n_slots):
    smem = pl.BlockSpec(memory_space=pltpu.SMEM)
    return pl.pallas_call(
        _slot_table_kernel,
        in_specs=[smem, smem],
        out_specs=smem,
        out_shape=jax.ShapeDtypeStruct((n_slots,), jnp.int32),
        name="slot_table",
    )(pos0, pos1)


def _routing_tables(idx, cnt, n):
    tm = EXPERT_TM
    e0, e1, r0, r1 = idx[0], idx[1], idx[2], idx[3]
    counts = cnt[:, 0].astype(jnp.int32)
    padded = ((counts + tm - 1) // tm) * tm
    pend = jnp.cumsum(padded)
    poff = pend - padded
    eid = jnp.arange(N_EXPERTS, dtype=jnp.int32)[:, None]
    pos0 = jnp.sum(jnp.where(e0[None, :] == eid, poff[:, None], 0), axis=0) + r0
    pos1 = jnp.sum(jnp.where(e1[None, :] == eid, poff[:, None], 0), axis=0) + r1
    n_slots = 2 * n + N_EXPERTS * tm
    slot_tok = _slot_table(pos0, pos1, n_slots)
    n_tiles = n_slots // tm
    nvalid = pend[-1] // tm
    tile = jnp.arange(n_tiles, dtype=jnp.int32)
    start = jnp.minimum(tile, nvalid - 1) * tm
    texp = jnp.sum((start[:, None] >= pend[None, :]).astype(jnp.int32), axis=1)
    texp = jnp.minimum(texp, N_EXPERTS - 1)
    return pos0, pos1, slot_tok, texp, nvalid.reshape(1).astype(jnp.int32)


def kernel(x, in_ln_g, in_ln_b, w_in, b_in, attn_rpb, hy_conv_w, hy_conv_b, hy_f_w1, hy_f_b1, hy_f_w2, hy_f_b2, hy_f_w3, hy_f_b3, hy_f_freq, hy_f_w4, hy_skip, cf_dw_w, cf_dw_b, cf_ln_g, cf_ln_b, w_attn_br, w_hy_br, w_cf_br, w_o, b_o, ln1_g, ln1_b, w_router, b_router, moe_w_gate, moe_w_up, moe_w_down, ln2_g, ln2_b):
    bsz, l, d = x.shape
    depth = w_in.shape[0]
    n = bsz * l
    mw = w_attn_br.shape[1]
    alpha = (2 * depth) ** 0.25
    c_qkv, c_hy, c_cf, c_gate = 0, 3 * mw, 6 * mw, 8 * mw

    w_in_f = w_in.astype(F32)
    b_in3 = b_in.astype(F32)[:, None, :]
    wa_b, wh_b, wc_b, wo_b = (w.astype(BF16) for w in (w_attn_br, w_hy_br, w_cf_br, w_o))
    wg_b, wu_b, wd_b = (w.astype(BF16) for w in (moe_w_gate, moe_w_up, moe_w_down))
    vec3 = lambda v: v.astype(F32)[:, None, :]
    bo3, cg3, cb3, g13, b13, g23, b23 = map(vec3, (b_o, cf_ln_g, cf_ln_b, ln1_g, ln1_b, ln2_g, ln2_b))
    tk = min(DFT_TK, l)
    fwd_np, inv_np = _dft_matrices(l, tk)
    fwd, inv = jnp.asarray(fwd_np), jnp.asarray(inv_np)

    h, hb = _in_ln(x.reshape(n, d), in_ln_g, in_ln_b)
    for layer in range(depth):
        proj = lambda c0, nc, dt, nm: _proj(hb, w_in_f, b_in3, layer=layer, col0=c0, ncols=nc,
                                            out_dtype=dt, name=nm)
        qkv = proj(c_qkv, 3 * mw, BF16, "proj_qkv")
        hy_in = proj(c_hy, 3 * mw, BF16, "proj_hyena")
        cf_in = proj(c_cf, 2 * mw, F32, "proj_conformer")
        gates = proj(c_gate, 3 * d, BF16, "proj_gates")

        y_a = _natten(qkv.reshape(bsz, l, 3 * mw), _natten_bias_table(attn_rpb[layer]))

        taps = _hy_filters(l, hy_f_w1[layer], hy_f_b1[layer], hy_f_w2[layer], hy_f_b2[layer],
                           hy_f_w3[layer], hy_f_b3[layer], hy_f_freq[layer], hy_f_w4[layer])
        spec = _hy_spectrum(fwd, taps, tk)
        u = hy_in.reshape(bsz, l, 3 * mw)
        cw, cb_ = hy_conv_w[layer].astype(F32), hy_conv_b[layer].astype(F32)[None]
        z1 = _long_conv(u, 0, True, u, 1, cw, cb_, hy_skip[layer, 0], spec, 0, fwd, inv, tk)
        y_h = _long_conv(z1, 0, False, u, 2, cw, cb_, hy_skip[layer, 1], spec, 1, fwd, inv, tk)

        y_c = _cf_conv(cf_in.reshape(bsz, l, 2 * mw), cf_dw_w[layer], cf_dw_b[layer])

        h = _merge(y_a.reshape(n, mw), y_h.reshape(n, mw), y_c.reshape(n, mw), gates, h,
                   wa_b, wh_b, wc_b, wo_b, bo3, cg3, cb3, g13, b13, layer, alpha)

        idx, wgt, cnt = _router(h, w_router, b_router)
        pos0, pos1, slot_tok, texp, nvalid = _routing_tables(idx, cnt, n)
        ys = _experts(h, texp, nvalid, slot_tok, wg_b, wu_b, wd_b, layer)
        h, hb = _combine(ys, h, pos0, pos1, wgt[0].reshape(n, 1), wgt[1].reshape(n, 1),
                         g23, b23, layer, alpha)
    return h.reshape(bsz, l, d)
```

```python
import functools
import math

import numpy as np
import jax
import jax.numpy as jnp
from jax import lax
from jax.experimental import pallas as pl
from jax.experimental.pallas import tpu as pltpu

F32 = jnp.float32
BF16 = jnp.bfloat16

GRID_W = 64
NA_HEAD_DIM = 64
NA_KH = 8
NA_KW = 16
HY_ORDER = 2
HY_POS_DIM = 33
HY_FAST_DECAY = 0.3
HY_SLOW_DECAY = 1.5
HY_DECAY_TARGET = 1e-2
CF_K = 31
N_GROUPS = 4
EXPERTS_PER_GROUP = 4
N_EXPERTS = N_GROUPS * EXPERTS_PER_GROUP
LN_EPS = 1e-5
NEG_INF = -1e30

LANES = 128
V7X_VMEM_BYTES = 64 * 1024 * 1024
MIB = 1024 * 1024

ROW_TILE = 256
MM_TM = 1024
MM_TN = 768
DFT_TK = 512
CONV_CB = 256
EXPERT_TM = 256
GATHER_UNROLL = 8
NATTEN_BATCH = 8


def _cparams(n_axes, vmem_mib, row_gather=False):
    assert vmem_mib * MIB < V7X_VMEM_BYTES
    return pltpu.CompilerParams(dimension_semantics=("arbitrary",) * n_axes,
                                vmem_limit_bytes=vmem_mib * MIB,
                                disable_bounds_checks=row_gather)


def _layer_norm(x, g, b):
    mu = jnp.mean(x, axis=-1, keepdims=True)
    xc = x - mu
    var = jnp.mean(xc * xc, axis=-1, keepdims=True)
    return xc * lax.rsqrt(var + LN_EPS) * g + b


def _sigmoid(x):
    return 1.0 / (1.0 + jnp.exp(-x))


def _in_ln_kernel(x_ref, g_ref, b_ref, h_ref, hb_ref):
    y = _layer_norm(x_ref[...], g_ref[...], b_ref[...])
    h_ref[...] = y
    hb_ref[...] = y.astype(BF16)


def _in_ln(x2, g, b):
    n, d = x2.shape
    row = pl.BlockSpec((ROW_TILE, d), lambda i: (i, 0))
    vec = pl.BlockSpec((1, d), lambda i: (0, 0))
    return pl.pallas_call(
        _in_ln_kernel,
        grid=(n // ROW_TILE,),
        in_specs=[row, vec, vec],
        out_specs=[row, row],
        out_shape=[jax.ShapeDtypeStruct((n, d), F32), jax.ShapeDtypeStruct((n, d), BF16)],
        compiler_params=_cparams(1, 32),
        name="in_ln",
    )(x2, g.reshape(1, d), b.reshape(1, d))


def _proj_kernel(x_ref, w_ref, b_ref, o_ref, wb_ref):
    @pl.when(pl.program_id(1) == 0)
    def _():
        wb_ref[...] = w_ref[...].astype(BF16)

    acc = jnp.dot(x_ref[...], wb_ref[...], preferred_element_type=F32)
    o_ref[...] = (acc + b_ref[...]).astype(o_ref.dtype)


def _proj(x, w, bias, *, layer, col0, ncols, out_dtype, name):
    m, k = x.shape
    tm = min(MM_TM, m)
    tn = MM_TN
    assert m % tm == 0 and ncols % tn == 0 and col0 % tn == 0
    jb = col0 // tn
    return pl.pallas_call(
        _proj_kernel,
        grid=(ncols // tn, m // tm),
        in_specs=[pl.BlockSpec((tm, k), lambda j, i: (i, 0)),
                  pl.BlockSpec((None, k, tn), lambda j, i: (layer, 0, j + jb)),
                  pl.BlockSpec((None, 1, tn), lambda j, i: (layer, 0, j + jb))],
        out_specs=pl.BlockSpec((tm, tn), lambda j, i: (i, j)),
        out_shape=jax.ShapeDtypeStruct((m, ncols), out_dtype),
        scratch_shapes=[pltpu.VMEM((k, tn), BF16)],
        compiler_params=_cparams(2, 48),
        name=name,
    )(x, w, bias)


def _natten_bias_table(rpb):
    qc = np.arange(GRID_W)
    kc = np.arange(GRID_W)
    cs = np.clip(qc - NA_KW // 2, 0, GRID_W - NA_KW)
    valid = (kc[None, :] >= cs[:, None]) & (kc[None, :] < cs[:, None] + NA_KW)
    col_off = np.clip(kc[None, :] - qc[:, None], -(NA_KW - 1), NA_KW - 1) + (NA_KW - 1)
    row_off = np.arange(NA_KH)[None, :] - np.arange(NA_KH)[:, None] + (NA_KH - 1)
    row_sel = (row_off[:, :, None] == np.arange(2 * NA_KH - 1)).astype(np.float32)
    col_sel = (col_off[:, :, None] == np.arange(2 * NA_KW - 1)).astype(np.float32)
    b = jnp.einsum("hrc,djr,qkc->dhqjk", rpb.astype(F32), row_sel, col_sel,
                   precision=lax.Precision.HIGHEST)
    b = jnp.where(jnp.asarray(valid)[None, None, :, None, :], b, NEG_INF)
    h = rpb.shape[0]
    return b.reshape(NA_KH, h, GRID_W, NA_KH * GRID_W)


def _natten_kernel(q_ref, k_ref, v_ref, bias_ref, o_ref, s_ref, p_ref, *, rows, batch):
    win = NA_KH * GRID_W
    pair_rows = 2 * GRID_W
    lane = lax.broadcasted_iota(jnp.int32, (GRID_W, 2 * NA_HEAD_DIM), 1)
    first = lane < NA_HEAD_DIM
    scale = jnp.asarray(NA_HEAD_DIM ** -0.5, BF16)
    window_start = lambda r: min(max(r - NA_KH // 2, 0), rows - NA_KH)
    for r0 in range(0, rows, batch):
        for i in range(batch):
            r = r0 + i
            rs = window_start(r)
            q = q_ref[0, r * GRID_W:(r + 1) * GRID_W, :] * scale
            zero = jnp.zeros_like(q)
            qm = jnp.concatenate([jnp.where(first, q, zero), jnp.where(first, zero, q)], axis=0)
            kw = k_ref[0, rs * GRID_W:rs * GRID_W + win, :]
            s = lax.dot_general(qm, kw, (((1,), (1,)), ((), ())), preferred_element_type=F32)
            s_ref[i * pair_rows:(i + 1) * pair_rows, :] = s + bias_ref[r - rs].reshape(pair_rows, win)
        s = s_ref[...]
        e = jnp.exp(s - jnp.max(s, axis=-1, keepdims=True))
        p_ref[...] = (e / jnp.sum(e, axis=-1, keepdims=True)).astype(BF16)
        for i in range(batch):
            r = r0 + i
            rs = window_start(r)
            vw = v_ref[0, rs * GRID_W:rs * GRID_W + win, :]
            o = jnp.dot(p_ref[i * pair_rows:(i + 1) * pair_rows, :], vw, preferred_element_type=F32)
            o_ref[0, r * GRID_W:(r + 1) * GRID_W, :] = (
                jnp.where(first, o[:GRID_W], o[GRID_W:]).astype(o_ref.dtype))


def _natten(qkv, bias_tab):
    b, l, w3 = qkv.shape
    w = w3 // 3
    pair = 2 * NA_HEAD_DIM
    npairs = w // pair
    rows = l // GRID_W
    batch = NATTEN_BATCH
    assert rows >= NA_KH and rows % batch == 0
    blk = lambda off: pl.BlockSpec((1, l, pair), lambda bi, hp: (bi, 0, hp + off))
    return pl.pallas_call(
        functools.partial(_natten_kernel, rows=rows, batch=batch),
        grid=(b, npairs),
        in_specs=[blk(0), blk(npairs), blk(2 * npairs),
                  pl.BlockSpec((NA_KH, 2, GRID_W, NA_KH * GRID_W), lambda bi, hp: (0, hp, 0, 0))],
        out_specs=pl.BlockSpec((1, l, pair), lambda bi, hp: (bi, 0, hp)),
        out_shape=jax.ShapeDtypeStruct((b, l, w), BF16),
        scratch_shapes=[pltpu.VMEM((batch * 2 * GRID_W, NA_KH * GRID_W), F32),
                        pltpu.VMEM((batch * 2 * GRID_W, NA_KH * GRID_W), BF16)],
        compiler_params=_cparams(2, 32),
        name="natten",
    )(qkv, qkv, qkv, bias_tab)


@functools.lru_cache(maxsize=None)
def _dft_matrices(l, tk):
    n2 = 2 * l
    k = np.arange(l, dtype=np.int64)
    n = np.arange(l, dtype=np.int64)
    ang = 2.0 * np.pi * ((k[:, None] * n[None, :]) % n2).astype(np.float64) / n2
    f_re = np.cos(ang)
    f_im = -np.sin(ang)
    f_im[0, :] = np.cos(np.pi * n)
    g_re = (2.0 / n2) * np.cos(ang).T
    g_re[:, 0] = 1.0 / n2
    g_im = -(2.0 / n2) * np.sin(ang).T
    g_im[:, 0] = np.cos(np.pi * n) / n2
    kt = l // tk
    fwd = np.stack([f_re.reshape(kt, tk, l), f_im.reshape(kt, tk, l)], axis=1).reshape(2 * l, l)
    inv = np.stack([g_re.reshape(l, kt, tk), g_im.reshape(l, kt, tk)], axis=2).reshape(l, 2 * l)
    return np.asarray(fwd, dtype=BF16), np.asarray(inv, dtype=BF16)


def _hy_filter_kernel(z_ref, t_ref, dl_ref, w1_ref, b1_ref, w2_ref, b2_ref, w3_ref, b3_ref,
                      fr_ref, w4_ref, o_ref, *, half):
    hp = lax.Precision.HIGHEST
    fr = fr_ref[...]
    h = jnp.sin(fr * (jnp.dot(z_ref[...], w1_ref[...], precision=hp, preferred_element_type=F32) + b1_ref[...]))
    h = jnp.sin(fr * (jnp.dot(h, w2_ref[...], precision=hp, preferred_element_type=F32) + b2_ref[...]))
    h = jnp.sin(fr * (jnp.dot(h, w3_ref[...], precision=hp, preferred_element_type=F32) + b3_ref[...]))
    h = jnp.dot(h, w4_ref[...], precision=hp, preferred_element_type=F32)
    h = h * jnp.exp(-t_ref[...] * dl_ref[...])
    tl, nc = h.shape
    row = lax.broadcasted_iota(jnp.int32, (tl, nc), 0) + pl.program_id(0) * tl
    col = lax.broadcasted_iota(jnp.int32, (tl, nc), 1)
    h = jnp.where((row == 0) & (col >= half), 0.0, h)
    o_ref[...] = h.astype(o_ref.dtype)


def _pad2(a, r, c):
    return jnp.pad(a.astype(F32), ((0, r - a.shape[0]), (0, c - a.shape[1])))


def _hy_filters(l, w1, b1, w2, b2, w3, b3, freq, w4):
    t = jnp.linspace(0.0, 1.0, l, dtype=F32)[:, None]
    bands = (HY_POS_DIM - 1) // 2
    w = 2.0 * math.pi * jnp.arange(l, dtype=F32)[:, None] / l
    f = jnp.linspace(1e-4, bands - 1, bands, dtype=F32)[None, :]
    z = jnp.concatenate([t, jnp.cos(f * w), -jnp.sin(f * w)], axis=-1)
    nc = w4.shape[1]
    max_decay = math.log(HY_DECAY_TARGET) / HY_FAST_DECAY
    min_decay = math.log(HY_DECAY_TARGET) / HY_SLOW_DECAY
    deltas = jnp.abs(jnp.linspace(min_decay, max_decay, nc, dtype=F32))[None, :]
    hid = LANES
    tl = min(256, l)
    full = lambda r, c: pl.BlockSpec((r, c), lambda i: (0, 0))
    return pl.pallas_call(
        functools.partial(_hy_filter_kernel, half=nc // 2),
        grid=(l // tl,),
        in_specs=[pl.BlockSpec((tl, hid), lambda i: (i, 0)), pl.BlockSpec((tl, 1), lambda i: (i, 0)),
                  full(1, nc), full(hid, hid), full(1, hid), full(hid, hid), full(1, hid),
                  full(hid, hid), full(1, hid), full(1, hid), full(hid, nc)],
        out_specs=pl.BlockSpec((tl, nc), lambda i: (i, 0)),
        out_shape=jax.ShapeDtypeStruct((l, nc), BF16),
        compiler_params=_cparams(1, 32),
        name="hy_filter",
    )(_pad2(z, l, hid), t, deltas, _pad2(w1, hid, hid), _pad2(b1[None], 1, hid),
      _pad2(w2, hid, hid), _pad2(b2[None], 1, hid), _pad2(w3, hid, hid), _pad2(b3[None], 1, hid),
      _pad2(freq[None], 1, hid), _pad2(w4, hid, nc))


def _spectrum_kernel(f_ref, tf_ref, tb_ref, o_ref, *, tk):
    sf = jnp.dot(f_ref[...], tf_ref[...], preferred_element_type=F32)
    sb = jnp.dot(f_ref[...], tb_ref[...], preferred_element_type=F32)
    im = sf[tk:] - sb[tk:]
    nyq = (lax.broadcasted_iota(jnp.int32, im.shape, 0) == 0) & (pl.program_id(0) == 0)
    o_ref[0:tk, :] = sf[:tk] + sb[:tk]
    o_ref[tk:2 * tk, :] = jnp.where(nyq, sf[tk:] + sb[tk:], im)


def _hy_spectrum(fwd, taps, tk):
    l2, l = fwd.shape
    c = taps.shape[1] // (2 * HY_ORDER)
    return pl.pallas_call(
        functools.partial(_spectrum_kernel, tk=tk),
        grid=(l2 // (2 * tk), HY_ORDER),
        in_specs=[pl.BlockSpec((2 * tk, l), lambda ki, o: (ki, 0)),
                  pl.BlockSpec((l, c), lambda ki, o: (0, o)),
                  pl.BlockSpec((l, c), lambda ki, o: (0, HY_ORDER + o))],
        out_specs=pl.BlockSpec((2 * tk, c), lambda ki, o: (ki, o)),
        out_shape=jax.ShapeDtypeStruct((l2, HY_ORDER * c), F32),
        compiler_params=_cparams(2, 40),
        name="hy_spectrum",
    )(fwd, taps, taps)


def _short_conv3(u_ref, w_ref, b_ref, o_ref):
    l, c = o_ref.shape
    row = lax.broadcasted_iota(jnp.int32, (l, CONV_CB), 0)
    for c0 in range(0, c, CONV_CB):
        u = u_ref[0, :, c0:c0 + CONV_CB].astype(F32)
        prev = jnp.where(row == 0, 0.0, pltpu.roll(u, 1, 0))
        nxt = jnp.where(row == l - 1, 0.0, pltpu.roll(u, l - 1, 0))
        w = w_ref[:, c0:c0 + CONV_CB]
        out = w[0:1] * prev + w[1:2] * u + w[2:3] * nxt + b_ref[:, c0:c0 + CONV_CB]
        o_ref[:, c0:c0 + CONV_CB] = out.astype(o_ref.dtype)


def _long_conv_kernel(x_ref, gate_ref, xw_ref, xb_ref, gw_ref, gb_ref, skip_ref, f_ref, g_ref, h_ref,
                      o_ref, acc_ref, xs_ref, gs_ref, *, tk, conv_x):
    kt = pl.program_id(1)

    @pl.when(kt == 0)
    def _():
        acc_ref[...] = jnp.zeros_like(acc_ref)
        if conv_x:
            _short_conv3(x_ref, xw_ref, xb_ref, xs_ref)
        else:
            xs_ref[...] = x_ref[0]
        _short_conv3(gate_ref, gw_ref, gb_ref, gs_ref)

    z = jnp.dot(f_ref[...], xs_ref[...], preferred_element_type=F32)
    zr, zi = z[:tk], z[tk:]
    hr, hi = h_ref[0:tk, :], h_ref[tk:2 * tk, :]
    nyq = (lax.broadcasted_iota(jnp.int32, zr.shape, 0) == 0) & (kt == 0)
    yr = zr * hr - jnp.where(nyq, 0.0, zi * hi)
    yi = jnp.where(nyq, zi * hi, zr * hi + zi * hr)
    y = jnp.concatenate([yr, yi], axis=0).astype(BF16)
    acc_ref[...] += jnp.dot(g_ref[...], y, preferred_element_type=F32)

    @pl.when(kt == pl.num_programs(1) - 1)
    def _():
        o_ref[0] = (gs_ref[...].astype(F32) * (acc_ref[...] + xs_ref[...].astype(F32) * skip_ref[...])
                    ).astype(o_ref.dtype)


def _long_conv(x_arr, x_blk, conv_x, u, gate_blk, conv_w, conv_b, skip, spec, order, fwd, inv, tk):
    b, l, _ = x_arr.shape
    c = skip.shape[-1]
    kt = l // tk
    once = pl.Buffered(1)
    xcol = x_blk if conv_x else 0
    return pl.pallas_call(
        functools.partial(_long_conv_kernel, tk=tk, conv_x=conv_x),
        grid=(b, kt),
        in_specs=[pl.BlockSpec((1, l, c), lambda bi, ki: (bi, 0, x_blk), pipeline_mode=once),
                  pl.BlockSpec((1, l, c), lambda bi, ki: (bi, 0, gate_blk), pipeline_mode=once),
                  pl.BlockSpec((3, c), lambda bi, ki: (0, xcol)),
                  pl.BlockSpec((1, c), lambda bi, ki: (0, xcol)),
                  pl.BlockSpec((3, c), lambda bi, ki: (0, gate_blk)),
                  pl.BlockSpec((1, c), lambda bi, ki: (0, gate_blk)),
                  pl.BlockSpec((1, c), lambda bi, ki: (0, 0)),
                  pl.BlockSpec((2 * tk, l), lambda bi, ki: (ki, 0)),
                  pl.BlockSpec((l, 2 * tk), lambda bi, ki: (0, ki)),
                  pl.BlockSpec((2 * tk, c), lambda bi, ki: (ki, order))],
        out_specs=pl.BlockSpec((1, l, c), lambda bi, ki: (bi, 0, 0)),
        out_shape=jax.ShapeDtypeStruct((b, l, c), BF16),
        scratch_shapes=[pltpu.VMEM((l, c), F32), pltpu.VMEM((l, c), BF16), pltpu.VMEM((l, c), BF16)],
        compiler_params=_cparams(2, 56),
        name=f"hy_long_conv{order}",
    )(x_arr, u, conv_w, conv_b, conv_w, conv_b, skip.astype(F32).reshape(1, c), fwd, inv, spec)


def _cf_conv_kernel(a_ref, g_ref, w_ref, b_ref, o_ref, zpad_ref, zsh_ref, *, chunk):
    l = a_ref.shape[1]
    pad = 16
    sub = 8
    z = a_ref[0] * _sigmoid(g_ref[0])
    zeros = jnp.zeros((pad, z.shape[1]), F32)
    zpad_ref[0:pad, :] = zeros
    zpad_ref[pad:pad + l, :] = z
    zpad_ref[pad + l:pad + l + pad, :] = zeros
    half = CF_K // 2
    nsh = zsh_ref.shape[0]
    for s in range(sub):
        zsh_ref[...] = zpad_ref[s:s + nsh, :]
        taps = [k for k in range(CF_K) if (pad - half + k) % sub == s]
        for c in range(l // chunk):
            if s == 0:
                acc = jnp.broadcast_to(b_ref[...], (chunk, z.shape[1]))
            else:
                acc = o_ref[0, c * chunk:(c + 1) * chunk, :]
            for k in taps:
                r0 = c * chunk + (pad - half + k) - s
                acc = acc + w_ref[k:k + 1, :] * zsh_ref[r0:r0 + chunk, :]
            o_ref[0, c * chunk:(c + 1) * chunk, :] = acc


def _cf_conv(cf_in, w, bias):
    b, l, w2 = cf_in.shape
    wd = w2 // 2
    cb = CONV_CB
    nb = wd // cb
    chunk = min(256, l)
    return pl.pallas_call(
        functools.partial(_cf_conv_kernel, chunk=chunk),
        grid=(b, nb),
        in_specs=[pl.BlockSpec((1, l, cb), lambda bi, ci: (bi, 0, ci)),
                  pl.BlockSpec((1, l, cb), lambda bi, ci: (bi, 0, ci + nb)),
                  pl.BlockSpec((CF_K, cb), lambda bi, ci: (0, ci)),
                  pl.BlockSpec((1, cb), lambda bi, ci: (0, ci))],
        out_specs=pl.BlockSpec((1, l, cb), lambda bi, ci: (bi, 0, ci)),
        out_shape=jax.ShapeDtypeStruct((b, l, wd), F32),
        scratch_shapes=[pltpu.VMEM((l + 32, cb), F32), pltpu.VMEM((l + 24, cb), F32)],
        compiler_params=_cparams(2, 32),
        name="cf_conv",
    )(cf_in, cf_in, w.astype(F32), bias.astype(F32).reshape(1, wd))


def _merge_kernel(ya_ref, yh_ref, yc_ref, ga_ref, gh_ref, gc_ref, h_ref, wa_ref, wh_ref, wc_ref,
                  wo_ref, bo_ref, cg_ref, cb_ref, g1_ref, b1_ref, o_ref, *, alpha):
    yc = _layer_norm(yc_ref[...], cg_ref[...], cb_ref[...])
    yc = (yc * _sigmoid(yc)).astype(BF16)
    m = _sigmoid(ga_ref[...].astype(F32)) * jnp.dot(ya_ref[...], wa_ref[...], preferred_element_type=F32)
    m = m + _sigmoid(gh_ref[...].astype(F32)) * jnp.dot(yh_ref[...], wh_ref[...], preferred_element_type=F32)
    m = m + _sigmoid(gc_ref[...].astype(F32)) * jnp.dot(yc, wc_ref[...], preferred_element_type=F32)
    mix = jnp.dot(m.astype(BF16), wo_ref[...], preferred_element_type=F32) + bo_ref[...]
    o_ref[...] = _layer_norm(alpha * h_ref[...] + mix, g1_ref[...], b1_ref[...])


def _merge(ya, yh, yc, gates, h, wa, wh, wc, wo, bo, cg, cb, g1, b1, layer, alpha):
    n, d = h.shape
    w = ya.shape[1]
    tm = ROW_TILE
    row = lambda c: pl.BlockSpec((tm, c), lambda i: (i, 0))
    gate = lambda j: pl.BlockSpec((tm, d), lambda i: (i, j))
    once = pl.Buffered(1)
    wspec = lambda r, c: pl.BlockSpec((None, r, c), lambda i: (layer, 0, 0), pipeline_mode=once)
    vec = lambda c: pl.BlockSpec((None, 1, c), lambda i: (layer, 0, 0))
    return pl.pallas_call(
        functools.partial(_merge_kernel, alpha=alpha),
        grid=(n // tm,),
        in_specs=[row(w), row(w), row(w), gate(0), gate(1), gate(2), row(d),
                  wspec(w, d), wspec(w, d), wspec(w, d), wspec(d, d),
                  vec(d), vec(w), vec(w), vec(d), vec(d)],
        out_specs=row(d),
        out_shape=jax.ShapeDtypeStruct((n, d), F32),
        compiler_params=_cparams(1, 48),
        name="merge",
    )(ya, yh, yc, gates, gates, gates, h, wa, wh, wc, wo, bo, cg, cb, g1, b1)


def _first_max(vals):
    m = vals[0]
    for v in vals[1:]:
        m = jnp.maximum(m, v)
    idx = jnp.full(m.shape, len(vals) - 1, jnp.int32)
    for j in range(len(vals) - 2, -1, -1):
        idx = jnp.where(vals[j] == m, j, idx)
    return m, idx


def _top2(vals):
    m1, i1 = _first_max(vals)
    rest = [jnp.where(i1 == j, -1.0, v) for j, v in enumerate(vals)]
    m2, i2 = _first_max(rest)
    return m1, i1, m2, i2


def _router_kernel(h_ref, wr_ref, br_ref, idx_ref, wgt_ref, cnt_ref, carry_ref):
    i = pl.program_id(0)

    @pl.when(i == 0)
    def _():
        carry_ref[...] = jnp.zeros_like(carry_ref)

    h = h_ref[...]
    h_hi = h.astype(BF16)
    h_lo = (h - h_hi.astype(F32)).astype(BF16)
    w = wr_ref[...]
    w_hi = w.astype(BF16)
    w_lo = (w - w_hi.astype(F32)).astype(BF16)
    logits = (jnp.dot(h_hi, w_hi, preferred_element_type=F32) + jnp.dot(h_hi, w_lo, preferred_element_type=F32)
              + jnp.dot(h_lo, w_hi, preferred_element_type=F32)) + br_ref[...]
    lt = logits.T[:N_EXPERTS]
    tm = lt.shape[1]
    mx = jnp.max(lt, axis=0, keepdims=True)
    ex = jnp.exp(lt - mx)
    probs = ex / jnp.sum(ex, axis=0, keepdims=True)
    p = [probs[e:e + 1, :] for e in range(N_EXPERTS)]
    scores = []
    for g in range(N_GROUPS):
        a, _, b, _ = _top2(p[g * EXPERTS_PER_GROUP:(g + 1) * EXPERTS_PER_GROUP])
        scores.append(a + b)
    _, g_sel = _first_max(scores)
    pg = []
    for j in range(EXPERTS_PER_GROUP):
        v = p[(N_GROUPS - 1) * EXPERTS_PER_GROUP + j]
        for g in range(N_GROUPS - 2, -1, -1):
            v = jnp.where(g_sel == g, p[g * EXPERTS_PER_GROUP + j], v)
        pg.append(v)
    p1, i1, p2, i2 = _top2(pg)
    den = p1 + p2
    e0 = g_sel * EXPERTS_PER_GROUP + i1
    e1 = g_sel * EXPERTS_PER_GROUP + i2

    erow = lax.broadcasted_iota(jnp.int32, (N_EXPERTS, tm), 0)
    oh0 = (erow == e0).astype(F32)
    oh1 = (erow == e1).astype(F32)
    both = oh0 + oh1
    before = (lax.broadcasted_iota(jnp.int32, (tm, tm), 0) < lax.broadcasted_iota(jnp.int32, (tm, tm), 1))
    cum = jnp.dot(both.astype(BF16), before.astype(BF16), preferred_element_type=F32) + carry_ref[:, 0:1]
    r0 = jnp.sum(oh0 * cum, axis=0, keepdims=True)
    r1 = jnp.sum(oh1 * cum, axis=0, keepdims=True)
    carry_ref[...] = carry_ref[...] + jnp.sum(both, axis=1, keepdims=True)
    cnt_ref[...] = carry_ref[...]

    zi = jnp.zeros((4, tm), jnp.int32)
    idx_ref[...] = jnp.concatenate([e0, e1, r0.astype(jnp.int32), r1.astype(jnp.int32), zi], axis=0)
    zf = jnp.zeros((6, tm), F32)
    wgt_ref[...] = jnp.concatenate([p1 / den, p2 / den, zf], axis=0)


def _router(h, w_router, b_router):
    n, d = h.shape
    tm = ROW_TILE
    wr = _pad2(w_router, d, LANES)
    br = _pad2(b_router[None], 1, LANES)
    return pl.pallas_call(
        _router_kernel,
        grid=(n // tm,),
        in_specs=[pl.BlockSpec((tm, d), lambda i: (i, 0)),
                  pl.BlockSpec((d, LANES), lambda i: (0, 0)),
                  pl.BlockSpec((1, LANES), lambda i: (0, 0))],
        out_specs=[pl.BlockSpec((8, tm), lambda i: (0, i)),
                   pl.BlockSpec((8, tm), lambda i: (0, i)),
                   pl.BlockSpec((N_EXPERTS, LANES), lambda i: (0, 0))],
        out_shape=[jax.ShapeDtypeStruct((8, n), jnp.int32),
                   jax.ShapeDtypeStruct((8, n), F32),
                   jax.ShapeDtypeStruct((N_EXPERTS, LANES), F32)],
        scratch_shapes=[pltpu.VMEM((N_EXPERTS, LANES), F32)],
        compiler_params=_cparams(1, 32),
        name="router",
    )(h, wr, br)


def _expert_kernel(texp_ref, nv_ref, stok_ref, h_hbm, wg_ref, wu_ref, wd_ref, o_ref, xbuf, sem, *, tm):
    i = pl.program_id(0)
    nv = nv_ref[0]
    nbuf = xbuf.shape[0]

    def row_copy(tok, slot, r):
        return pltpu.make_async_copy(h_hbm.at[pl.ds(tok, 1), :], xbuf.at[slot, pl.ds(r, 1), :], sem.at[slot])

    def wait_tile(slot):
        pltpu.make_async_copy(h_hbm.at[pl.ds(0, tm), :], xbuf.at[slot], sem.at[slot]).wait()

    @pl.when(i == 0)
    def _():
        second = jnp.minimum(1, nv - 1) * tm

        def body(r, carry):
            row_copy(stok_ref[r], 0, r).start()
            row_copy(stok_ref[second + r], 1, r).start()
            return carry
        lax.fori_loop(0, tm, body, 0, unroll=GATHER_UNROLL)

    @pl.when(i < nv)
    def _():
        slot = i % nbuf
        wait_tile(slot)
        x = xbuf[slot].astype(BF16)
        g = jnp.dot(x, wg_ref[...], preferred_element_type=F32)
        u = jnp.dot(x, wu_ref[...], preferred_element_type=F32)
        hid = (g * _sigmoid(g) * u).astype(BF16)
        o_ref[...] = jnp.dot(hid, wd_ref[...], preferred_element_type=F32)
        base = jnp.minimum(i + 2, nv - 1) * tm
        nxt = (i + 2) % nbuf
        for r in range(tm):
            row_copy(stok_ref[base + r], nxt, r).start()

    @pl.when(i == nv - 1)
    def _():
        wait_tile(nv % nbuf)
        wait_tile((nv + 1) % nbuf)

    @pl.when(i >= nv)
    def _():
        o_ref[...] = jnp.zeros_like(o_ref)


def _experts(h, texp, nvalid, slot_tok, wg, wu, wd, layer):
    n, d = h.shape
    de = wg.shape[-1]
    tm = EXPERT_TM
    n_tiles = slot_tok.shape[0] // tm
    grid_spec = pltpu.PrefetchScalarGridSpec(
        num_scalar_prefetch=3,
        grid=(n_tiles,),
        in_specs=[pl.BlockSpec(memory_space=pl.ANY),
                  pl.BlockSpec((None, None, d, de), lambda i, te, nv, st: (layer, te[i], 0, 0)),
                  pl.BlockSpec((None, None, d, de), lambda i, te, nv, st: (layer, te[i], 0, 0)),
                  pl.BlockSpec((None, None, de, d), lambda i, te, nv, st: (layer, te[i], 0, 0))],
        out_specs=pl.BlockSpec((tm, d), lambda i, te, nv, st: (i, 0)),
        scratch_shapes=[pltpu.VMEM((3, tm, d), F32), pltpu.SemaphoreType.DMA((3,))],
    )
    return pl.pallas_call(
        functools.partial(_expert_kernel, tm=tm),
        grid_spec=grid_spec,
        out_shape=jax.ShapeDtypeStruct((n_tiles * tm, d), F32),
        compiler_params=_cparams(1, 48, row_gather=True),
        name="experts",
    )(texp, nvalid, slot_tok, h, wg, wu, wd)


def _combine_kernel(p0_ref, p1_ref, ys_hbm, h_ref, w0_ref, w1_ref, g_ref, b_ref, o_ref, ob_ref,
                    gbuf, sem, *, tm, alpha):
    i = pl.program_id(0)
    nt = pl.num_programs(0)
    nbuf = gbuf.shape[0]

    def start_rows(t, slot, r):
        pltpu.make_async_copy(ys_hbm.at[pl.ds(p0_ref[t], 1), :], gbuf.at[slot, 0, pl.ds(r, 1), :],
                              sem.at[slot]).start()
        pltpu.make_async_copy(ys_hbm.at[pl.ds(p1_ref[t], 1), :], gbuf.at[slot, 1, pl.ds(r, 1), :],
                              sem.at[slot]).start()

    def wait_tile(slot):
        for j in range(2):
            pltpu.make_async_copy(ys_hbm.at[pl.ds(0, tm), :], gbuf.at[slot, j], sem.at[slot]).wait()

    @pl.when(i == 0)
    def _():
        second = jnp.minimum(1, nt - 1) * tm

        def body(r, carry):
            start_rows(r, 0, r)
            start_rows(second + r, 1, r)
            return carry
        lax.fori_loop(0, tm, body, 0, unroll=GATHER_UNROLL)

    slot = i % nbuf
    wait_tile(slot)
    y = w0_ref[...] * gbuf[slot, 0] + w1_ref[...] * gbuf[slot, 1]
    out = _layer_norm(alpha * h_ref[...] + y, g_ref[...], b_ref[...])
    o_ref[...] = out
    ob_ref[...] = out.astype(BF16)
    base = jnp.minimum(i + 2, nt - 1) * tm
    nxt = (i + 2) % nbuf
    for r in range(tm):
        start_rows(base + r, nxt, r)

    @pl.when(i == nt - 1)
    def _():
        wait_tile(nt % nbuf)
        wait_tile((nt + 1) % nbuf)


def _combine(ys, h, pos0, pos1, w0, w1, g2, b2, layer, alpha):
    n, d = h.shape
    tm = ROW_TILE
    row = pl.BlockSpec((tm, d), lambda i, a, b: (i, 0))
    col = pl.BlockSpec((tm, 1), lambda i, a, b: (i, 0))
    vec = pl.BlockSpec((None, 1, d), lambda i, a, b: (layer, 0, 0))
    grid_spec = pltpu.PrefetchScalarGridSpec(
        num_scalar_prefetch=2,
        grid=(n // tm,),
        in_specs=[pl.BlockSpec(memory_space=pl.ANY), row, col, col, vec, vec],
        out_specs=[row, row],
        scratch_shapes=[pltpu.VMEM((3, 2, tm, d), F32), pltpu.SemaphoreType.DMA((3,))],
    )
    return pl.pallas_call(
        functools.partial(_combine_kernel, tm=tm, alpha=alpha),
        grid_spec=grid_spec,
        out_shape=[jax.ShapeDtypeStruct((n, d), F32), jax.ShapeDtypeStruct((n, d), BF16)],
        compiler_params=_cparams(1, 44, row_gather=True),
        name="combine",
    )(pos0, pos1, ys, h, w0, w1, g2, b2)


def _slot_table_kernel(p0_ref, p1_ref, o_ref):
    n_slots = o_ref.shape[0]
    n = p0_ref.shape[0]

    def zero(i, carry):
        o_ref[i] = 0
        return carry
    lax.fori_loop(0, n_slots, zero, 0, unroll=GATHER_UNROLL)

    def body(t, carry):
        o_ref[p0_ref[t]] = t
        o_ref[p1_ref[t]] = t
        return carry
    lax.fori_loop(0, n, body, 0, unroll=GATHER_UNROLL)


def _slot_table(pos0, pos1, n_slots):
    smem = pl.BlockSpec(memory_space=pltpu.SMEM)
    return pl.pallas_call(
        _slot_table_kernel,
        in_specs=[smem, smem],
        out_specs=smem,
        out_shape=jax.ShapeDtypeStruct((n_slots,), jnp.int32),
        name="slot_table",
    )(pos0, pos1)


def _routing_tables(idx, cnt, n):
    tm = EXPERT_TM
    e0, e1, r0, r1 = idx[0], idx[1], idx[2], idx[3]
    counts = cnt[:, 0].astype(jnp.int32)
    padded = ((counts + tm - 1) // tm) * tm
    pend = jnp.cumsum(padded)
    poff = pend - padded
    eid = jnp.arange(N_EXPERTS, dtype=jnp.int32)[:, None]
    pos0 = jnp.sum(jnp.where(e0[None, :] == eid, poff[:, None], 0), axis=0) + r0
    pos1 = jnp.sum(jnp.where(e1[None, :] == eid, poff[:, None], 0), axis=0) + r1
    n_slots = 2 * n + N_EXPERTS * tm
    slot_tok = _slot_table(pos0, pos1, n_slots)
    n_tiles = n_slots // tm
    nvalid = pend[-1] // tm
    tile = jnp.arange(n_tiles, dtype=jnp.int32)
    start = jnp.minimum(tile, nvalid - 1) * tm
    texp = jnp.sum((start[:, None] >= pend[None, :]).astype(jnp.int32), axis=1)
    texp = jnp.minimum(texp, N_EXPERTS - 1)
    return pos0, pos1, slot_tok, texp, nvalid.reshape(1).astype(jnp.int32)


def kernel(x, in_ln_g, in_ln_b, w_in, b_in, attn_rpb, hy_conv_w, hy_conv_b, hy_f_w1, hy_f_b1, hy_f_w2, hy_f_b2, hy_f_w3, hy_f_b3, hy_f_freq, hy_f_w4, hy_skip, cf_dw_w, cf_dw_b, cf_ln_g, cf_ln_b, w_attn_br, w_hy_br, w_cf_br, w_o, b_o, ln1_g, ln1_b, w_router, b_router, moe_w_gate, moe_w_up, moe_w_down, ln2_g, ln2_b):
    bsz, l, d = x.shape
    depth = w_in.shape[0]
    n = bsz * l
    mw = w_attn_br.shape[1]
    alpha = (2 * depth) ** 0.25
    c_qkv, c_hy, c_cf, c_gate = 0, 3 * mw, 6 * mw, 8 * mw

    w_in_f = w_in.astype(F32)
    b_in3 = b_in.astype(F32)[:, None, :]
    wa_b, wh_b, wc_b, wo_b = (w.astype(BF16) for w in (w_attn_br, w_hy_br, w_cf_br, w_o))
    wg_b, wu_b, wd_b = (w.astype(BF16) for w in (moe_w_gate, moe_w_up, moe_w_down))
    vec3 = lambda v: v.astype(F32)[:, None, :]
    bo3, cg3, cb3, g13, b13, g23, b23 = map(vec3, (b_o, cf_ln_g, cf_ln_b, ln1_g, ln1_b, ln2_g, ln2_b))
    tk = min(DFT_TK, l)
    fwd_np, inv_np = _dft_matrices(l, tk)
    fwd, inv = jnp.asarray(fwd_np), jnp.asarray(inv_np)

    h, hb = _in_ln(x.reshape(n, d), in_ln_g, in_ln_b)
    for layer in range(depth):
        proj = lambda c0, nc, dt, nm: _proj(hb, w_in_f, b_in3, layer=layer, col0=c0, ncols=nc,
                                            out_dtype=dt, name=nm)
        qkv = proj(c_qkv, 3 * mw, BF16, "proj_qkv")
        hy_in = proj(c_hy, 3 * mw, BF16, "proj_hyena")
        cf_in = proj(c_cf, 2 * mw, F32, "proj_conformer")
        gates = proj(c_gate, 3 * d, BF16, "proj_gates")

        y_a = _natten(qkv.reshape(bsz, l, 3 * mw), _natten_bias_table(attn_rpb[layer]))

        taps = _hy_filters(l, hy_f_w1[layer], hy_f_b1[layer], hy_f_w2[layer], hy_f_b2[layer],
                           hy_f_w3[layer], hy_f_b3[layer], hy_f_freq[layer], hy_f_w4[layer])
        spec = _hy_spectrum(fwd, taps, tk)
        u = hy_in.reshape(bsz, l, 3 * mw)
        cw, cb_ = hy_conv_w[layer].astype(F32), hy_conv_b[layer].astype(F32)[None]
        z1 = _long_conv(u, 0, True, u, 1, cw, cb_, hy_skip[layer, 0], spec, 0, fwd, inv, tk)
        y_h = _long_conv(z1, 0, False, u, 2, cw, cb_, hy_skip[layer, 1], spec, 1, fwd, inv, tk)

        y_c = _cf_conv(cf_in.reshape(bsz, l, 2 * mw), cf_dw_w[layer], cf_dw_b[layer])

        h = _merge(y_a.reshape(n, mw), y_h.reshape(n, mw), y_c.reshape(n, mw), gates, h,
                   wa_b, wh_b, wc_b, wo_b, bo3, cg3, cb3, g13, b13, layer, alpha)

        idx, wgt, cnt = _router(h, w_router, b_router)
        pos0, pos1, slot_tok, texp, nvalid = _routing_tables(idx, cnt, n)
        ys = _experts(h, texp, nvalid, slot_tok, wg_b, wu_b, wd_b, layer)
        h, hb = _combine(ys, h, pos0, pos1, wgt[0].reshape(n, 1), wgt[1].reshape(n, 1),
                         g23, b23, layer, alpha)
    return h.reshape(bsz, l, d)
```

```python
import functools
import math

import numpy as np
import jax
import jax.numpy as jnp
from jax import lax
from jax.experimental import pallas as pl
from jax.experimental.pallas import tpu as pltpu

F32 = jnp.float32
BF16 = jnp.bfloat16

GRID_W = 64
NA_HEAD_DIM = 64
NA_KH = 8
NA_KW = 16
HY_ORDER = 2
HY_POS_DIM = 33
HY_FAST_DECAY = 0.3
HY_SLOW_DECAY = 1.5
HY_DECAY_TARGET = 1e-2
CF_K = 31
N_GROUPS = 4
EXPERTS_PER_GROUP = 4
N_EXPERTS = N_GROUPS * EXPERTS_PER_GROUP
LN_EPS = 1e-5
NEG_INF = -1e30

LANES = 128
V7X_VMEM_BYTES = 64 * 1024 * 1024
MIB = 1024 * 1024

ROW_TILE = 256
MM_TM = 1024
MM_TN = 768
DFT_TK = 512
CONV_CB = 256
EXPERT_TM = 256
GATHER_UNROLL = 8
NATTEN_BATCH = 8


def _cparams(n_axes, vmem_mib, row_gather=False):
    assert vmem_mib * MIB < V7X_VMEM_BYTES
    return pltpu.CompilerParams(dimension_semantics=("arbitrary",) * n_axes,
                                vmem_limit_bytes=vmem_mib * MIB,
                                disable_bounds_checks=row_gather)


def _layer_norm(x, g, b):
    mu = jnp.mean(x, axis=-1, keepdims=True)
    xc = x - mu
    var = jnp.mean(xc * xc, axis=-1, keepdims=True)
    return xc * lax.rsqrt(var + LN_EPS) * g + b


def _sigmoid(x):
    return 1.0 / (1.0 + jnp.exp(-x))


def _in_ln_kernel(x_ref, g_ref, b_ref, h_ref, hb_ref):
    y = _layer_norm(x_ref[...], g_ref[...], b_ref[...])
    h_ref[...] = y
    hb_ref[...] = y.astype(BF16)


def _in_ln(x2, g, b):
    n, d = x2.shape
    row = pl.BlockSpec((ROW_TILE, d), lambda i: (i, 0))
    vec = pl.BlockSpec((1, d), lambda i: (0, 0))
    return pl.pallas_call(
        _in_ln_kernel,
        grid=(n // ROW_TILE,),
        in_specs=[row, vec, vec],
        out_specs=[row, row],
        out_shape=[jax.ShapeDtypeStruct((n, d), F32), jax.ShapeDtypeStruct((n, d), BF16)],
        compiler_params=_cparams(1, 32),
        name="in_ln",
    )(x2, g.reshape(1, d), b.reshape(1, d))


def _proj_kernel(x_ref, w_ref, b_ref, o_ref, wb_ref):
    @pl.when(pl.program_id(1) == 0)
    def _():
        wb_ref[...] = w_ref[...].astype(BF16)

    acc = jnp.dot(x_ref[...], wb_ref[...], preferred_element_type=F32)
    o_ref[...] = (acc + b_ref[...]).astype(o_ref.dtype)


def _proj(x, w, bias, *, layer, col0, ncols, out_dtype, name):
    m, k = x.shape
    tm = min(MM_TM, m)
    tn = MM_TN
    assert m % tm == 0 and ncols % tn == 0 and col0 % tn == 0
    jb = col0 // tn
    return pl.pallas_call(
        _proj_kernel,
        grid=(ncols // tn, m // tm),
        in_specs=[pl.BlockSpec((tm, k), lambda j, i: (i, 0)),
                  pl.BlockSpec((None, k, tn), lambda j, i: (layer, 0, j + jb)),
                  pl.BlockSpec((None, 1, tn), lambda j, i: (layer, 0, j + jb))],
        out_specs=pl.BlockSpec((tm, tn), lambda j, i: (i, j)),
        out_shape=jax.ShapeDtypeStruct((m, ncols), out_dtype),
        scratch_shapes=[pltpu.VMEM((k, tn), BF16)],
        compiler_params=_cparams(2, 48),
        name=name,
    )(x, w, bias)


def _natten_bias_table(rpb):
    qc = np.arange(GRID_W)
    kc = np.arange(GRID_W)
    cs = np.clip(qc - NA_KW // 2, 0, GRID_W - NA_KW)
    valid = (kc[None, :] >= cs[:, None]) & (kc[None, :] < cs[:, None] + NA_KW)
    col_off = np.clip(kc[None, :] - qc[:, None], -(NA_KW - 1), NA_KW - 1) + (NA_KW - 1)
    row_off = np.arange(NA_KH)[None, :] - np.arange(NA_KH)[:, None] + (NA_KH - 1)
    row_sel = (row_off[:, :, None] == np.arange(2 * NA_KH - 1)).astype(np.float32)
    col_sel = (col_off[:, :, None] == np.arange(2 * NA_KW - 1)).astype(np.float32)
    b = jnp.einsum("hrc,djr,qkc->dhqjk", rpb.astype(F32), row_sel, col_sel,
                   precision=lax.Precision.HIGHEST)
    b = jnp.where(jnp.asarray(valid)[None, None, :, None, :], b, NEG_INF)
    h = rpb.shape[0]
    return b.reshape(NA_KH, h, GRID_W, NA_KH * GRID_W)


def _natten_kernel(q_ref, k_ref, v_ref, bias_ref, o_ref, s_ref, p_ref, *, rows, batch):
    win = NA_KH * GRID_W
    pair_rows = 2 * GRID_W
    lane = lax.broadcasted_iota(jnp.int32, (GRID_W, 2 * NA_HEAD_DIM), 1)
    first = lane < NA_HEAD_DIM
    scale = jnp.asarray(NA_HEAD_DIM ** -0.5, BF16)
    window_start = lambda r: min(max(r - NA_KH // 2, 0), rows - NA_KH)
    for r0 in range(0, rows, batch):
        for i in range(batch):
            r = r0 + i
            rs = window_start(r)
            q = q_ref[0, r * GRID_W:(r + 1) * GRID_W, :] * scale
            zero = jnp.zeros_like(q)
            qm = jnp.concatenate([jnp.where(first, q, zero), jnp.where(first, zero, q)], axis=0)
            kw = k_ref[0, rs * GRID_W:rs * GRID_W + win, :]
            s = lax.dot_general(qm, kw, (((1,), (1,)), ((), ())), preferred_element_type=F32)
            s_ref[i * pair_rows:(i + 1) * pair_rows, :] = s + bias_ref[r - rs].reshape(pair_rows, win)
        s = s_ref[...]
        e = jnp.exp(s - jnp.max(s, axis=-1, keepdims=True))
        p_ref[...] = (e / jnp.sum(e, axis=-1, keepdims=True)).astype(BF16)
        for i in range(batch):
            r = r0 + i
            rs = window_start(r)
            vw = v_ref[0, rs * GRID_W:rs * GRID_W + win, :]
            o = jnp.dot(p_ref[i * pair_rows:(i + 1) * pair_rows, :], vw, preferred_element_type=F32)
            o_ref[0, r * GRID_W:(r + 1) * GRID_W, :] = (
                jnp.where(first, o[:GRID_W], o[GRID_W:]).astype(o_ref.dtype))


def _natten(qkv, bias_tab):
    b, l, w3 = qkv.shape
    w = w3 // 3
    pair = 2 * NA_HEAD_DIM
    npairs = w // pair
    rows = l // GRID_W
    batch = NATTEN_BATCH
    assert rows >= NA_KH and rows % batch == 0
    blk = lambda off: pl.BlockSpec((1, l, pair), lambda bi, hp: (bi, 0, hp + off))
    return pl.pallas_call(
        functools.partial(_natten_kernel, rows=rows, batch=batch),
        grid=(b, npairs),
        in_specs=[blk(0), blk(npairs), blk(2 * npairs),
                  pl.BlockSpec((NA_KH, 2, GRID_W, NA_KH * GRID_W), lambda bi, hp: (0, hp, 0, 0))],
        out_specs=pl.BlockSpec((1, l, pair), lambda bi, hp: (bi, 0, hp)),
        out_shape=jax.ShapeDtypeStruct((b, l, w), BF16),
        scratch_shapes=[pltpu.VMEM((batch * 2 * GRID_W, NA_KH * GRID_W), F32),
                        pltpu.VMEM((batch * 2 * GRID_W, NA_KH * GRID_W), BF16)],
        compiler_params=_cparams(2, 32),
        name="natten",
    )(qkv, qkv, qkv, bias_tab)


@functools.lru_cache(maxsize=None)
def _dft_matrices(l, tk):
    n2 = 2 * l
    k = np.arange(l, dtype=np.int64)
    n = np.arange(l, dtype=np.int64)
    ang = 2.0 * np.pi * ((k[:, None] * n[None, :]) % n2).astype(np.float64) / n2
    f_re = np.cos(ang)
    f_im = -np.sin(ang)
    f_im[0, :] = np.cos(np.pi * n)
    g_re = (2.0 / n2) * np.cos(ang).T
    g_re[:, 0] = 1.0 / n2
    g_im = -(2.0 / n2) * np.sin(ang).T
    g_im[:, 0] = np.cos(np.pi * n) / n2
    kt = l // tk
    fwd = np.stack([f_re.reshape(kt, tk, l), f_im.reshape(kt, tk, l)], axis=1).reshape(2 * l, l)
    inv = np.stack([g_re.reshape(l, kt, tk), g_im.reshape(l, kt, tk)], axis=2).reshape(l, 2 * l)
    return np.asarray(fwd, dtype=BF16), np.asarray(inv, dtype=BF16)


def _hy_filter_kernel(z_ref, t_ref, dl_ref, w1_ref, b1_ref, w2_ref, b2_ref, w3_ref, b3_ref,
                      fr_ref, w4_ref, o_ref, *, half):
    hp = lax.Precision.HIGHEST
    fr = fr_ref[...]
    h = jnp.sin(fr * (jnp.dot(z_ref[...], w1_ref[...], precision=hp, preferred_element_type=F32) + b1_ref[...]))
    h = jnp.sin(fr * (jnp.dot(h, w2_ref[...], precision=hp, preferred_element_type=F32) + b2_ref[...]))
    h = jnp.sin(fr * (jnp.dot(h, w3_ref[...], precision=hp, preferred_element_type=F32) + b3_ref[...]))
    h = jnp.dot(h, w4_ref[...], precision=hp, preferred_element_type=F32)
    h = h * jnp.exp(-t_ref[...] * dl_ref[...])
    tl, nc = h.shape
    row = lax.broadcasted_iota(jnp.int32, (tl, nc), 0) + pl.program_id(0) * tl
    col = lax.broadcasted_iota(jnp.int32, (tl, nc), 1)
    h = jnp.where((row == 0) & (col >= half), 0.0, h)
    o_ref[...] = h.astype(o_ref.dtype)


def _pad2(a, r, c):
    return jnp.pad(a.astype(F32), ((0, r - a.shape[0]), (0, c - a.shape[1])))


def _hy_filters(l, w1, b1, w2, b2, w3, b3, freq, w4):
    t = jnp.linspace(0.0, 1.0, l, dtype=F32)[:, None]
    bands = (HY_POS_DIM - 1) // 2
    w = 2.0 * math.pi * jnp.arange(l, dtype=F32)[:, None] / l
    f = jnp.linspace(1e-4, bands - 1, bands, dtype=F32)[None, :]
    z = jnp.concatenate([t, jnp.cos(f * w), -jnp.sin(f * w)], axis=-1)
    nc = w4.shape[1]
    max_decay = math.log(HY_DECAY_TARGET) / HY_FAST_DECAY
    min_decay = math.log(HY_DECAY_TARGET) / HY_SLOW_DECAY
    deltas = jnp.abs(jnp.linspace(min_decay, max_decay, nc, dtype=F32))[None, :]
    hid = LANES
    tl = min(256, l)
    full = lambda r, c: pl.BlockSpec((r, c), lambda i: (0, 0))
    return pl.pallas_call(
        functools.partial(_hy_filter_kernel, half=nc // 2),
        grid=(l // tl,),
        in_specs=[pl.BlockSpec((tl, hid), lambda i: (i, 0)), pl.BlockSpec((tl, 1), lambda i: (i, 0)),
                  full(1, nc), full(hid, hid), full(1, hid), full(hid, hid), full(1, hid),
                  full(hid, hid), full(1, hid), full(1, hid), full(hid, nc)],
        out_specs=pl.BlockSpec((tl, nc), lambda i: (i, 0)),
        out_shape=jax.ShapeDtypeStruct((l, nc), BF16),
        compiler_params=_cparams(1, 32),
        name="hy_filter",
    )(_pad2(z, l, hid), t, deltas, _pad2(w1, hid, hid), _pad2(b1[None], 1, hid),
      _pad2(w2, hid, hid), _pad2(b2[None], 1, hid), _pad2(w3, hid, hid), _pad2(b3[None], 1, hid),
      _pad2(freq[None], 1, hid), _pad2(w4, hid, nc))


def _spectrum_kernel(f_ref, tf_ref, tb_ref, o_ref, *, tk):
    sf = jnp.dot(f_ref[...], tf_ref[...], preferred_element_type=F32)
    sb = jnp.dot(f_ref[...], tb_ref[...], preferred_element_type=F32)
    im = sf[tk:] - sb[tk:]
    nyq = (lax.broadcasted_iota(jnp.int32, im.shape, 0) == 0) & (pl.program_id(0) == 0)
    o_ref[0:tk, :] = sf[:tk] + sb[:tk]
    o_ref[tk:2 * tk, :] = jnp.where(nyq, sf[tk:] + sb[tk:], im)


def _hy_spectrum(fwd, taps, tk):
    l2, l = fwd.shape
    c = taps.shape[1] // (2 * HY_ORDER)
    return pl.pallas_call(
        functools.partial(_spectrum_kernel, tk=tk),
        grid=(l2 // (2 * tk), HY_ORDER),
        in_specs=[pl.BlockSpec((2 * tk, l), lambda ki, o: (ki, 0)),
                  pl.BlockSpec((l, c), lambda ki, o: (0, o)),
                  pl.BlockSpec((l, c), lambda ki, o: (0, HY_ORDER + o))],
        out_specs=pl.BlockSpec((2 * tk, c), lambda ki, o: (ki, o)),
        out_shape=jax.ShapeDtypeStruct((l2, HY_ORDER * c), F32),
        compiler_params=_cparams(2, 40),
        name="hy_spectrum",
    )(fwd, taps, taps)


def _short_conv3(u_ref, w_ref, b_ref, o_ref):
    l, c = o_ref.shape
    row = lax.broadcasted_iota(jnp.int32, (l, CONV_CB), 0)
    for c0 in range(0, c, CONV_CB):
        u = u_ref[0, :, c0:c0 + CONV_CB].astype(F32)
        prev = jnp.where(row == 0, 0.0, pltpu.roll(u, 1, 0))
        nxt = jnp.where(row == l - 1, 0.0, pltpu.roll(u, l - 1, 0))
        w = w_ref[:, c0:c0 + CONV_CB]
        out = w[0:1] * prev + w[1:2] * u + w[2:3] * nxt + b_ref[:, c0:c0 + CONV_CB]
        o_ref[:, c0:c0 + CONV_CB] = out.astype(o_ref.dtype)


def _long_conv_kernel(x_ref, gate_ref, xw_ref, xb_ref, gw_ref, gb_ref, skip_ref, f_ref, g_ref, h_ref,
                      o_ref, acc_ref, xs_ref, gs_ref, *, tk, conv_x):
    kt = pl.program_id(1)

    @pl.when(kt == 0)
    def _():
        acc_ref[...] = jnp.zeros_like(acc_ref)
        if conv_x:
            _short_conv3(x_ref, xw_ref, xb_ref, xs_ref)
        else:
            xs_ref[...] = x_ref[0]
        _short_conv3(gate_ref, gw_ref, gb_ref, gs_ref)

    z = jnp.dot(f_ref[...], xs_ref[...], preferred_element_type=F32)
    zr, zi = z[:tk], z[tk:]
    hr, hi = h_ref[0:tk, :], h_ref[tk:2 * tk, :]
    nyq = (lax.broadcasted_iota(jnp.int32, zr.shape, 0) == 0) & (kt == 0)
    yr = zr * hr - jnp.where(nyq, 0.0, zi * hi)
    yi = jnp.where(nyq, zi * hi, zr * hi + zi * hr)
    y = jnp.concatenate([yr, yi], axis=0).astype(BF16)
    acc_ref[...] += jnp.dot(g_ref[...], y, preferred_element_type=F32)

    @pl.when(kt == pl.num_programs(1) - 1)
    def _():
        o_ref[0] = (gs_ref[...].astype(F32) * (acc_ref[...] + xs_ref[...].astype(F32) * skip_ref[...])
                    ).astype(o_ref.dtype)


def _long_conv(x_arr, x_blk, conv_x, u, gate_blk, conv_w, conv_b, skip, spec, order, fwd, inv, tk):
    b, l, _ = x_arr.shape
    c = skip.shape[-1]
    kt = l // tk
    once = pl.Buffered(1)
    xcol = x_blk if conv_x else 0
    return pl.pallas_call(
        functools.partial(_long_conv_kernel, tk=tk, conv_x=conv_x),
        grid=(b, kt),
        in_specs=[pl.BlockSpec((1, l, c), lambda bi, ki: (bi, 0, x_blk), pipeline_mode=once),
                  pl.BlockSpec((1, l, c), lambda bi, ki: (bi, 0, gate_blk), pipeline_mode=once),
                  pl.BlockSpec((3, c), lambda bi, ki: (0, xcol)),
                  pl.BlockSpec((1, c), lambda bi, ki: (0, xcol)),
                  pl.BlockSpec((3, c), lambda bi, ki: (0, gate_blk)),
                  pl.BlockSpec((1, c), lambda bi, ki: (0, gate_blk)),
                  pl.BlockSpec((1, c), lambda bi, ki: (0, 0)),
                  pl.BlockSpec((2 * tk, l), lambda bi, ki: (ki, 0)),
                  pl.BlockSpec((l, 2 * tk), lambda bi, ki: (0, ki)),
                  pl.BlockSpec((2 * tk, c), lambda bi, ki: (ki, order))],
        out_specs=pl.BlockSpec((1, l, c), lambda bi, ki: (bi, 0, 0)),
        out_shape=jax.ShapeDtypeStruct((b, l, c), BF16),
        scratch_shapes=[pltpu.VMEM((l, c), F32), pltpu.VMEM((l, c), BF16), pltpu.VMEM((l, c), BF16)],
        compiler_params=_cparams(2, 56),
        name=f"hy_long_conv{order}",
    )(x_arr, u, conv_w, conv_b, conv_w, conv_b, skip.astype(F32).reshape(1, c), fwd, inv, spec)


def _cf_conv_kernel(a_ref, g_ref, w_ref, b_ref, o_ref, zpad_ref, zsh_ref, *, chunk):
    l = a_ref.shape[1]
    pad = 16
    sub = 8
    z = a_ref[0] * _sigmoid(g_ref[0])
    zeros = jnp.zeros((pad, z.shape[1]), F32)
    zpad_ref[0:pad, :] = zeros
    zpad_ref[pad:pad + l, :] = z
    zpad_ref[pad + l:pad + l + pad, :] = zeros
    half = CF_K // 2
    nsh = zsh_ref.shape[0]
    for s in range(sub):
        zsh_ref[...] = zpad_ref[s:s + nsh, :]
        taps = [k for k in range(CF_K) if (pad - half + k) % sub == s]
        for c in range(l // chunk):
            if s == 0:
                acc = jnp.broadcast_to(b_ref[...], (chunk, z.shape[1]))
            else:
                acc = o_ref[0, c * chunk:(c + 1) * chunk, :]
            for k in taps:
                r0 = c * chunk + (pad - half + k) - s
                acc = acc + w_ref[k:k + 1, :] * zsh_ref[r0:r0 + chunk, :]
            o_ref[0, c * chunk:(c + 1) * chunk, :] = acc


def _cf_conv(cf_in, w, bias):
    b, l, w2 = cf_in.shape
    wd = w2 // 2
    cb = CONV_CB
    nb = wd // cb
    chunk = min(256, l)
    return pl.pallas_call(
        functools.partial(_cf_conv_kernel, chunk=chunk),
        grid=(b, nb),
        in_specs=[pl.BlockSpec((1, l, cb), lambda bi, ci: (bi, 0, ci)),
                  pl.BlockSpec((1, l, cb), lambda bi, ci: (bi, 0, ci + nb)),
                  pl.BlockSpec((CF_K, cb), lambda bi, ci: (0, ci)),
                  pl.BlockSpec((1, cb), lambda bi, ci: (0, ci))],
        out_specs=pl.BlockSpec((1, l, cb), lambda bi, ci: (bi, 0, ci)),
        out_shape=jax.ShapeDtypeStruct((b, l, wd), F32),
        scratch_shapes=[pltpu.VMEM((l + 32, cb), F32), pltpu.VMEM((l + 24, cb), F32)],
        compiler_params=_cparams(2, 32),
        name="cf_conv",
    )(cf_in, cf_in, w.astype(F32), bias.astype(F32).reshape(1, wd))


def _merge_kernel(ya_ref, yh_ref, yc_ref, ga_ref, gh_ref, gc_ref, h_ref, wa_ref, wh_ref, wc_ref,
                  wo_ref, bo_ref, cg_ref, cb_ref, g1_ref, b1_ref, o_ref, *, alpha):
    yc = _layer_norm(yc_ref[...], cg_ref[...], cb_ref[...])
    yc = (yc * _sigmoid(yc)).astype(BF16)
    m = _sigmoid(ga_ref[...].astype(F32)) * jnp.dot(ya_ref[...], wa_ref[...], preferred_element_type=F32)
    m = m + _sigmoid(gh_ref[...].astype(F32)) * jnp.dot(yh_ref[...], wh_ref[...], preferred_element_type=F32)
    m = m + _sigmoid(gc_ref[...].astype(F32)) * jnp.dot(yc, wc_ref[...], preferred_element_type=F32)
    mix = jnp.dot(m.astype(BF16), wo_ref[...], preferred_element_type=F32) + bo_ref[...]
    o_ref[...] = _layer_norm(alpha * h_ref[...] + mix, g1_ref[...], b1_ref[...])


def _merge(ya, yh, yc, gates, h, wa, wh, wc, wo, bo, cg, cb, g1, b1, layer, alpha):
    n, d = h.shape
    w = ya.shape[1]
    tm = ROW_TILE
    row = lambda c: pl.BlockSpec((tm, c), lambda i: (i, 0))
    gate = lambda j: pl.BlockSpec((tm, d), lambda i: (i, j))
    once = pl.Buffered(1)
    wspec = lambda r, c: pl.BlockSpec((None, r, c), lambda i: (layer, 0, 0), pipeline_mode=once)
    vec = lambda c: pl.BlockSpec((None, 1, c), lambda i: (layer, 0, 0))
    return pl.pallas_call(
        functools.partial(_merge_kernel, alpha=alpha),
        grid=(n // tm,),
        in_specs=[row(w), row(w), row(w), gate(0), gate(1), gate(2), row(d),
                  wspec(w, d), wspec(w, d), wspec(w, d), wspec(d, d),
                  vec(d), vec(w), vec(w), vec(d), vec(d)],
        out_specs=row(d),
        out_shape=jax.ShapeDtypeStruct((n, d), F32),
        compiler_params=_cparams(1, 48),
        name="merge",
    )(ya, yh, yc, gates, gates, gates, h, wa, wh, wc, wo, bo, cg, cb, g1, b1)


def _first_max(vals):
    m = vals[0]
    for v in vals[1:]:
        m = jnp.maximum(m, v)
    idx = jnp.full(m.shape, len(vals) - 1, jnp.int32)
    for j in range(len(vals) - 2, -1, -1):
        idx = jnp.where(vals[j] == m, j, idx)
    return m, idx


def _top2(vals):
    m1, i1 = _first_max(vals)
    rest = [jnp.where(i1 == j, -1.0, v) for j, v in enumerate(vals)]
    m2, i2 = _first_max(rest)
    return m1, i1, m2, i2


def _router_kernel(h_ref, wr_ref, br_ref, idx_ref, wgt_ref, cnt_ref, carry_ref):
    i = pl.program_id(0)

    @pl.when(i == 0)
    def _():
        carry_ref[...] = jnp.zeros_like(carry_ref)

    h = h_ref[...]
    h_hi = h.astype(BF16)
    h_lo = (h - h_hi.astype(F32)).astype(BF16)
    w = wr_ref[...]
    w_hi = w.astype(BF16)
    w_lo = (w - w_hi.astype(F32)).astype(BF16)
    logits = (jnp.dot(h_hi, w_hi, preferred_element_type=F32) + jnp.dot(h_hi, w_lo, preferred_element_type=F32)
              + jnp.dot(h_lo, w_hi, preferred_element_type=F32)) + br_ref[...]
    lt = logits.T[:N_EXPERTS]
    tm = lt.shape[1]
    mx = jnp.max(lt, axis=0, keepdims=True)
    ex = jnp.exp(lt - mx)
    probs = ex / jnp.sum(ex, axis=0, keepdims=True)
    p = [probs[e:e + 1, :] for e in range(N_EXPERTS)]
    scores = []
    for g in range(N_GROUPS):
        a, _, b, _ = _top2(p[g * EXPERTS_PER_GROUP:(g + 1) * EXPERTS_PER_GROUP])
        scores.append(a + b)
    _, g_sel = _first_max(scores)
    pg = []
    for j in range(EXPERTS_PER_GROUP):
        v = p[(N_GROUPS - 1) * EXPERTS_PER_GROUP + j]
        for g in range(N_GROUPS - 2, -1, -1):
            v = jnp.where(g_sel == g, p[g * EXPERTS_PER_GROUP + j], v)
        pg.append(v)
    p1, i1, p2, i2 = _top2(pg)
    den = p1 + p2
    e0 = g_sel * EXPERTS_PER_GROUP + i1
    e1 = g_sel * EXPERTS_PER_GROUP + i2

    erow = lax.broadcasted_iota(jnp.int32, (N_EXPERTS, tm), 0)
    oh0 = (erow == e0).astype(F32)
    oh1 = (erow == e1).astype(F32)
    both = oh0 + oh1
    before = (lax.broadcasted_iota(jnp.int32, (tm, tm), 0) < lax.broadcasted_iota(jnp.int32, (tm, tm), 1))
    cum = jnp.dot(both.astype(BF16), before.astype(BF16), preferred_element_type=F32) + carry_ref[:, 0:1]
    r0 = jnp.sum(oh0 * cum, axis=0, keepdims=True)
    r1 = jnp.sum(oh1 * cum, axis=0, keepdims=True)
    carry_ref[...] = carry_ref[...] + jnp.sum(both, axis=1, keepdims=True)
    cnt_ref[...] = carry_ref[...]

    zi = jnp.zeros((4, tm), jnp.int32)
    idx_ref[...] = jnp.concatenate([e0, e1, r0.astype(jnp.int32), r1.astype(jnp.int32), zi], axis=0)
    zf = jnp.zeros((6, tm), F32)
    wgt_ref[...] = jnp.concatenate([p1 / den, p2 / den, zf], axis=0)


def _router(h, w_router, b_router):
    n, d = h.shape
    tm = ROW_TILE
    wr = _pad2(w_router, d, LANES)
    br = _pad2(b_router[None], 1, LANES)
    return pl.pallas_call(
        _router_kernel,
        grid=(n // tm,),
        in_specs=[pl.BlockSpec((tm, d), lambda i: (i, 0)),
                  pl.BlockSpec((d, LANES), lambda i: (0, 0)),
                  pl.BlockSpec((1, LANES), lambda i: (0, 0))],
        out_specs=[pl.BlockSpec((8, tm), lambda i: (0, i)),
                   pl.BlockSpec((8, tm), lambda i: (0, i)),
                   pl.BlockSpec((N_EXPERTS, LANES), lambda i: (0, 0))],
        out_shape=[jax.ShapeDtypeStruct((8, n), jnp.int32),
                   jax.ShapeDtypeStruct((8, n), F32),
                   jax.ShapeDtypeStruct((N_EXPERTS, LANES), F32)],
        scratch_shapes=[pltpu.VMEM((N_EXPERTS, LANES), F32)],
        compiler_params=_cparams(1, 32),
        name="router",
    )(h, wr, br)


def _expert_kernel(texp_ref, nv_ref, stok_ref, h_hbm, wg_ref, wu_ref, wd_ref, o_ref, xbuf, sem, *, tm):
    i = pl.program_id(0)
    nv = nv_ref[0]
    nbuf = xbuf.shape[0]

    def row_copy(tok, slot, r):
        return pltpu.make_async_copy(h_hbm.at[pl.ds(tok, 1), :], xbuf.at[slot, pl.ds(r, 1), :], sem.at[slot])

    def wait_tile(slot):
        pltpu.make_async_copy(h_hbm.at[pl.ds(0, tm), :], xbuf.at[slot], sem.at[slot]).wait()

    @pl.when(i == 0)
    def _():
        second = jnp.minimum(1, nv - 1) * tm

        def body(r, carry):
            row_copy(stok_ref[r], 0, r).start()
            row_copy(stok_ref[second + r], 1, r).start()
            return carry
        lax.fori_loop(0, tm, body, 0, unroll=GATHER_UNROLL)

    @pl.when(i < nv)
    def _():
        slot = i % nbuf
        wait_tile(slot)
        x = xbuf[slot].astype(BF16)
        g = jnp.dot(x, wg_ref[...], preferred_element_type=F32)
        u = jnp.dot(x, wu_ref[...], preferred_element_type=F32)
        hid = (g * _sigmoid(g) * u).astype(BF16)
        o_ref[...] = jnp.dot(hid, wd_ref[...], preferred_element_type=F32)
        base = jnp.minimum(i + 2, nv - 1) * tm
        nxt = (i + 2) % nbuf
        for r in range(tm):
            row_copy(stok_ref[base + r], nxt, r).start()

    @pl.when(i == nv - 1)
    def _():
        wait_tile(nv % nbuf)
        wait_tile((nv + 1) % nbuf)

    @pl.when(i >= nv)
    def _():
        o_ref[...] = jnp.zeros_like(o_ref)


def _experts(h, texp, nvalid, slot_tok, wg, wu, wd, layer):
    n, d = h.shape
    de = wg.shape[-1]
    tm = EXPERT_TM
    n_tiles = slot_tok.shape[0] // tm
    grid_spec = pltpu.PrefetchScalarGridSpec(
        num_scalar_prefetch=3,
        grid=(n_tiles,),
        in_specs=[pl.BlockSpec(memory_space=pl.ANY),
                  pl.BlockSpec((None, None, d, de), lambda i, te, nv, st: (layer, te[i], 0, 0)),
                  pl.BlockSpec((None, None, d, de), lambda i, te, nv, st: (layer, te[i], 0, 0)),
                  pl.BlockSpec((None, None, de, d), lambda i, te, nv, st: (layer, te[i], 0, 0))],
        out_specs=pl.BlockSpec((tm, d), lambda i, te, nv, st: (i, 0)),
        scratch_shapes=[pltpu.VMEM((3, tm, d), F32), pltpu.SemaphoreType.DMA((3,))],
    )
    return pl.pallas_call(
        functools.partial(_expert_kernel, tm=tm),
        grid_spec=grid_spec,
        out_shape=jax.ShapeDtypeStruct((n_tiles * tm, d), F32),
        compiler_params=_cparams(1, 48, row_gather=True),
        name="experts",
    )(texp, nvalid, slot_tok, h, wg, wu, wd)


def _combine_kernel(p0_ref, p1_ref, ys_hbm, h_ref, w0_ref, w1_ref, g_ref, b_ref, o_ref, ob_ref,
                    gbuf, sem, *, tm, alpha):
    i = pl.program_id(0)
    nt = pl.num_programs(0)
    nbuf = gbuf.shape[0]

    def start_rows(t, slot, r):
        pltpu.make_async_copy(ys_hbm.at[pl.ds(p0_ref[t], 1), :], gbuf.at[slot, 0, pl.ds(r, 1), :],
                              sem.at[slot]).start()
        pltpu.make_async_copy(ys_hbm.at[pl.ds(p1_ref[t], 1), :], gbuf.at[slot, 1, pl.ds(r, 1), :],
                              sem.at[slot]).start()

    def wait_tile(slot):
        for j in range(2):
            pltpu.make_async_copy(ys_hbm.at[pl.ds(0, tm), :], gbuf.at[slot, j], sem.at[slot]).wait()

    @pl.when(i == 0)
    def _():
        second = jnp.minimum(1, nt - 1) * tm

        def body(r, carry):
            start_rows(r, 0, r)
            start_rows(second + r, 1, r)
            return carry
        lax.fori_loop(0, tm, body, 0, unroll=GATHER_UNROLL)

    slot = i % nbuf
    wait_tile(slot)
    y = w0_ref[...] * gbuf[slot, 0] + w1_ref[...] * gbuf[slot, 1]
    out = _layer_norm(alpha * h_ref[...] + y, g_ref[...], b_ref[...])
    o_ref[...] = out
    ob_ref[...] = out.astype(BF16)
    base = jnp.minimum(i + 2, nt - 1) * tm
    nxt = (i + 2) % nbuf
    for r in range(tm):
        start_rows(base + r, nxt, r)

    @pl.when(i == nt - 1)
    def _():
        wait_tile(nt % nbuf)
        wait_tile((nt + 1) % nbuf)


def _combine(ys, h, pos0, pos1, w0, w1, g2, b2, layer, alpha):
    n, d = h.shape
    tm = ROW_TILE
    row = pl.BlockSpec((tm, d), lambda i, a, b: (i, 0))
    col = pl.BlockSpec((tm, 1), lambda i, a, b: (i, 0))
    vec = pl.BlockSpec((None, 1, d), lambda i, a, b: (layer, 0, 0))
    grid_spec = pltpu.PrefetchScalarGridSpec(
        num_scalar_prefetch=2,
        grid=(n // tm,),
        in_specs=[pl.BlockSpec(memory_space=pl.ANY), row, col, col, vec, vec],
        out_specs=[row, row],
        scratch_shapes=[pltpu.VMEM((3, 2, tm, d), F32), pltpu.SemaphoreType.DMA((3,))],
    )
    return pl.pallas_call(
        functools.partial(_combine_kernel, tm=tm, alpha=alpha),
        grid_spec=grid_spec,
        out_shape=[jax.ShapeDtypeStruct((n, d), F32), jax.ShapeDtypeStruct((n, d), BF16)],
        compiler_params=_cparams(1, 44, row_gather=True),
        name="combine",
    )(pos0, pos1, ys, h, w0, w1, g2, b2)


def _slot_table_kernel(p0_ref, p1_ref, zeros_hbm, o_hbm, tab_ref, sem):
    n = p0_ref.shape[0]
    fill = pltpu.make_async_copy(zeros_hbm, tab_ref, sem)
    fill.start()
    fill.wait()

    def body(t, carry):
        tab_ref[p0_ref[t]] = t
        tab_ref[p1_ref[t]] = t
        return carry
    lax.fori_loop(0, n, body, 0, unroll=GATHER_UNROLL)
    out = pltpu.make_async_copy(tab_ref, o_hbm, sem)
    out.start()
    out.wait()


def _slot_table(pos0, pos1, n_slots):
    smem = pl.BlockSpec(memory_space=pltpu.SMEM)
    hbm = pl.BlockSpec(memory_space=pl.ANY)
    return pl.pallas_call(
        _slot_table_kernel,
        in_specs=[smem, smem, hbm],
        out_specs=hbm,
        out_shape=jax.ShapeDtypeStruct((n_slots,), jnp.int32),
        scratch_shapes=[pltpu.SMEM((n_slots,), jnp.int32), pltpu.SemaphoreType.DMA(())],
        name="slot_table",
    )(pos0, pos1, jnp.zeros((n_slots,), jnp.int32))


def _routing_tables(idx, cnt, n):
    tm = EXPERT_TM
    e0, e1, r0, r1 = idx[0], idx[1], idx[2], idx[3]
    counts = cnt[:, 0].astype(jnp.int32)
    padded = ((counts + tm - 1) // tm) * tm
    pend = jnp.cumsum(padded)
    poff = pend - padded
    eid = jnp.arange(N_EXPERTS, dtype=jnp.int32)[:, None]
    pos0 = jnp.sum(jnp.where(e0[None, :] == eid, poff[:, None], 0), axis=0) + r0
    pos1 = jnp.sum(jnp.where(e1[None, :] == eid, poff[:, None], 0), axis=0) + r1
    n_slots = 2 * n + N_EXPERTS * tm
    slot_tok = _slot_table(pos0, pos1, n_slots)
    n_tiles = n_slots // tm
    nvalid = pend[-1] // tm
    tile = jnp.arange(n_tiles, dtype=jnp.int32)
    start = jnp.minimum(tile, nvalid - 1) * tm
    texp = jnp.sum((start[:, None] >= pend[None, :]).astype(jnp.int32), axis=1)
    texp = jnp.minimum(texp, N_EXPERTS - 1)
    return pos0, pos1, slot_tok, texp, nvalid.reshape(1).astype(jnp.int32)


def kernel(x, in_ln_g, in_ln_b, w_in, b_in, attn_rpb, hy_conv_w, hy_conv_b, hy_f_w1, hy_f_b1, hy_f_w2, hy_f_b2, hy_f_w3, hy_f_b3, hy_f_freq, hy_f_w4, hy_skip, cf_dw_w, cf_dw_b, cf_ln_g, cf_ln_b, w_attn_br, w_hy_br, w_cf_br, w_o, b_o, ln1_g, ln1_b, w_router, b_router, moe_w_gate, moe_w_up, moe_w_down, ln2_g, ln2_b):
    bsz, l, d = x.shape
    depth = w_in.shape[0]
    n = bsz * l
    mw = w_attn_br.shape[1]
    alpha = (2 * depth) ** 0.25
    c_qkv, c_hy, c_cf, c_gate = 0, 3 * mw, 6 * mw, 8 * mw

    w_in_f = w_in.astype(F32)
    b_in3 = b_in.astype(F32)[:, None, :]
    wa_b, wh_b, wc_b, wo_b = (w.astype(BF16) for w in (w_attn_br, w_hy_br, w_cf_br, w_o))
    wg_b, wu_b, wd_b = (w.astype(BF16) for w in (moe_w_gate, moe_w_up, moe_w_down))
    vec3 = lambda v: v.astype(F32)[:, None, :]
    bo3, cg3, cb3, g13, b13, g23, b23 = map(vec3, (b_o, cf_ln_g, cf_ln_b, ln1_g, ln1_b, ln2_g, ln2_b))
    tk = min(DFT_TK, l)
    fwd_np, inv_np = _dft_matrices(l, tk)
    fwd, inv = jnp.asarray(fwd_np), jnp.asarray(inv_np)

    h, hb = _in_ln(x.reshape(n, d), in_ln_g, in_ln_b)
    for layer in range(depth):
        proj = lambda c0, nc, dt, nm: _proj(hb, w_in_f, b_in3, layer=layer, col0=c0, ncols=nc,
                                            out_dtype=dt, name=nm)
        qkv = proj(c_qkv, 3 * mw, BF16, "proj_qkv")
        hy_in = proj(c_hy, 3 * mw, BF16, "proj_hyena")
        cf_in = proj(c_cf, 2 * mw, F32, "proj_conformer")
        gates = proj(c_gate, 3 * d, BF16, "proj_gates")

        y_a = _natten(qkv.reshape(bsz, l, 3 * mw), _natten_bias_table(attn_rpb[layer]))

        taps = _hy_filters(l, hy_f_w1[layer], hy_f_b1[layer], hy_f_w2[layer], hy_f_b2[layer],
                           hy_f_w3[layer], hy_f_b3[layer], hy_f_freq[layer], hy_f_w4[layer])
        spec = _hy_spectrum(fwd, taps, tk)
        u = hy_in.reshape(bsz, l, 3 * mw)
        cw, cb_ = hy_conv_w[layer].astype(F32), hy_conv_b[layer].astype(F32)[None]
        z1 = _long_conv(u, 0, True, u, 1, cw, cb_, hy_skip[layer, 0], spec, 0, fwd, inv, tk)
        y_h = _long_conv(z1, 0, False, u, 2, cw, cb_, hy_skip[layer, 1], spec, 1, fwd, inv, tk)

        y_c = _cf_conv(cf_in.reshape(bsz, l, 2 * mw), cf_dw_w[layer], cf_dw_b[layer])

        h = _merge(y_a.reshape(n, mw), y_h.reshape(n, mw), y_c.reshape(n, mw), gates, h,
                   wa_b, wh_b, wc_b, wo_b, bo3, cg3, cb3, g13, b13, layer, alpha)

        idx, wgt, cnt = _router(h, w_router, b_router)
        pos0, pos1, slot_tok, texp, nvalid = _routing_tables(idx, cnt, n)
        ys = _experts(h, texp, nvalid, slot_tok, wg_b, wu_b, wd_b, layer)
        h, hb = _combine(ys, h, pos0, pos1, wgt[0].reshape(n, 1), wgt[1].reshape(n, 1),
                         g23, b23, layer, alpha)
    return h.reshape(bsz, l, d)
```

```python
import functools
import math

import numpy as np
import jax
import jax.numpy as jnp
from jax import lax
from jax.experimental import pallas as pl
from jax.experimental.pallas import tpu as pltpu

F32 = jnp.float32
BF16 = jnp.bfloat16

GRID_W = 64
NA_HEAD_DIM = 64
NA_KH = 8
NA_KW = 16
HY_ORDER = 2
HY_POS_DIM = 33
HY_FAST_DECAY = 0.3
HY_SLOW_DECAY = 1.5
HY_DECAY_TARGET = 1e-2
CF_K = 31
N_GROUPS = 4
EXPERTS_PER_GROUP = 4
N_EXPERTS = N_GROUPS * EXPERTS_PER_GROUP
LN_EPS = 1e-5
NEG_INF = -1e30

LANES = 128
V7X_VMEM_BYTES = 64 * 1024 * 1024
MIB = 1024 * 1024

ROW_TILE = 256
MM_TM = 1024
MM_TN = 768
DFT_TK = 512
CONV_CB = 256
EXPERT_TM = 256
GATHER_UNROLL = 8
NATTEN_BATCH = 8


def _cparams(n_axes, vmem_mib, row_gather=False):
    assert vmem_mib * MIB < V7X_VMEM_BYTES
    return pltpu.CompilerParams(dimension_semantics=("arbitrary",) * n_axes,
                                vmem_limit_bytes=vmem_mib * MIB,
                                disable_bounds_checks=row_gather)


def _layer_norm(x, g, b):
    mu = jnp.mean(x, axis=-1, keepdims=True)
    xc = x - mu
    var = jnp.mean(xc * xc, axis=-1, keepdims=True)
    return xc * lax.rsqrt(var + LN_EPS) * g + b


def _sigmoid(x):
    return 1.0 / (1.0 + jnp.exp(-x))


def _in_ln_kernel(x_ref, g_ref, b_ref, h_ref, hb_ref):
    y = _layer_norm(x_ref[...], g_ref[...], b_ref[...])
    h_ref[...] = y
    hb_ref[...] = y.astype(BF16)


def _in_ln(x2, g, b):
    n, d = x2.shape
    row = pl.BlockSpec((ROW_TILE, d), lambda i: (i, 0))
    vec = pl.BlockSpec((1, d), lambda i: (0, 0))
    return pl.pallas_call(
        _in_ln_kernel,
        grid=(n // ROW_TILE,),
        in_specs=[row, vec, vec],
        out_specs=[row, row],
        out_shape=[jax.ShapeDtypeStruct((n, d), F32), jax.ShapeDtypeStruct((n, d), BF16)],
        compiler_params=_cparams(1, 32),
        name="in_ln",
    )(x2, g.reshape(1, d), b.reshape(1, d))


def _proj_kernel(x_ref, w_ref, b_ref, o_ref, wb_ref):
    @pl.when(pl.program_id(1) == 0)
    def _():
        wb_ref[...] = w_ref[...].astype(BF16)

    acc = jnp.dot(x_ref[...], wb_ref[...], preferred_element_type=F32)
    o_ref[...] = (acc + b_ref[...]).astype(o_ref.dtype)


def _proj(x, w, bias, *, layer, col0, ncols, out_dtype, name):
    m, k = x.shape
    tm = min(MM_TM, m)
    tn = MM_TN
    assert m % tm == 0 and ncols % tn == 0 and col0 % tn == 0
    jb = col0 // tn
    return pl.pallas_call(
        _proj_kernel,
        grid=(ncols // tn, m // tm),
        in_specs=[pl.BlockSpec((tm, k), lambda j, i: (i, 0)),
                  pl.BlockSpec((None, k, tn), lambda j, i: (layer, 0, j + jb)),
                  pl.BlockSpec((None, 1, tn), lambda j, i: (layer, 0, j + jb))],
        out_specs=pl.BlockSpec((tm, tn), lambda j, i: (i, j)),
        out_shape=jax.ShapeDtypeStruct((m, ncols), out_dtype),
        scratch_shapes=[pltpu.VMEM((k, tn), BF16)],
        compiler_params=_cparams(2, 48),
        name=name,
    )(x, w, bias)


def _natten_bias_table(rpb):
    qc = np.arange(GRID_W)
    kc = np.arange(GRID_W)
    cs = np.clip(qc - NA_KW // 2, 0, GRID_W - NA_KW)
    valid = (kc[None, :] >= cs[:, None]) & (kc[None, :] < cs[:, None] + NA_KW)
    col_off = np.clip(kc[None, :] - qc[:, None], -(NA_KW - 1), NA_KW - 1) + (NA_KW - 1)
    row_off = np.arange(NA_KH)[None, :] - np.arange(NA_KH)[:, None] + (NA_KH - 1)
    row_sel = (row_off[:, :, None] == np.arange(2 * NA_KH - 1)).astype(np.float32)
    col_sel = (col_off[:, :, None] == np.arange(2 * NA_KW - 1)).astype(np.float32)
    b = jnp.einsum("hrc,djr,qkc->dhqjk", rpb.astype(F32), row_sel, col_sel,
                   precision=lax.Precision.HIGHEST)
    b = jnp.where(jnp.asarray(valid)[None, None, :, None, :], b, NEG_INF)
    h = rpb.shape[0]
    return b.reshape(NA_KH, h, GRID_W, NA_KH * GRID_W)


def _natten_kernel(q_ref, k_ref, v_ref, bias_ref, o_ref, s_ref, p_ref, *, rows, batch):
    win = NA_KH * GRID_W
    pair_rows = 2 * GRID_W
    lane = lax.broadcasted_iota(jnp.int32, (GRID_W, 2 * NA_HEAD_DIM), 1)
    first = lane < NA_HEAD_DIM
    scale = jnp.asarray(NA_HEAD_DIM ** -0.5, BF16)
    window_start = lambda r: min(max(r - NA_KH // 2, 0), rows - NA_KH)
    for r0 in range(0, rows, batch):
        for i in range(batch):
            r = r0 + i
            rs = window_start(r)
            q = q_ref[0, r * GRID_W:(r + 1) * GRID_W, :] * scale
            zero = jnp.zeros_like(q)
            qm = jnp.concatenate([jnp.where(first, q, zero), jnp.where(first, zero, q)], axis=0)
            kw = k_ref[0, rs * GRID_W:rs * GRID_W + win, :]
            s = lax.dot_general(qm, kw, (((1,), (1,)), ((), ())), preferred_element_type=F32)
            s_ref[i * pair_rows:(i + 1) * pair_rows, :] = s + bias_ref[r - rs].reshape(pair_rows, win)
        s = s_ref[...]
        e = jnp.exp(s - jnp.max(s, axis=-1, keepdims=True))
        p_ref[...] = (e / jnp.sum(e, axis=-1, keepdims=True)).astype(BF16)
        for i in range(batch):
            r = r0 + i
            rs = window_start(r)
            vw = v_ref[0, rs * GRID_W:rs * GRID_W + win, :]
            o = jnp.dot(p_ref[i * pair_rows:(i + 1) * pair_rows, :], vw, preferred_element_type=F32)
            o_ref[0, r * GRID_W:(r + 1) * GRID_W, :] = (
                jnp.where(first, o[:GRID_W], o[GRID_W:]).astype(o_ref.dtype))


def _natten(qkv, bias_tab):
    b, l, w3 = qkv.shape
    w = w3 // 3
    pair = 2 * NA_HEAD_DIM
    npairs = w // pair
    rows = l // GRID_W
    batch = NATTEN_BATCH
    assert rows >= NA_KH and rows % batch == 0
    blk = lambda off: pl.BlockSpec((1, l, pair), lambda bi, hp: (bi, 0, hp + off))
    return pl.pallas_call(
        functools.partial(_natten_kernel, rows=rows, batch=batch),
        grid=(b, npairs),
        in_specs=[blk(0), blk(npairs), blk(2 * npairs),
                  pl.BlockSpec((NA_KH, 2, GRID_W, NA_KH * GRID_W), lambda bi, hp: (0, hp, 0, 0))],
        out_specs=pl.BlockSpec((1, l, pair), lambda bi, hp: (bi, 0, hp)),
        out_shape=jax.ShapeDtypeStruct((b, l, w), BF16),
        scratch_shapes=[pltpu.VMEM((batch * 2 * GRID_W, NA_KH * GRID_W), F32),
                        pltpu.VMEM((batch * 2 * GRID_W, NA_KH * GRID_W), BF16)],
        compiler_params=_cparams(2, 32),
        name="natten",
    )(qkv, qkv, qkv, bias_tab)


@functools.lru_cache(maxsize=None)
def _dft_matrices(l, tk):
    n2 = 2 * l
    k = np.arange(l, dtype=np.int64)
    n = np.arange(l, dtype=np.int64)
    ang = 2.0 * np.pi * ((k[:, None] * n[None, :]) % n2).astype(np.float64) / n2
    f_re = np.cos(ang)
    f_im = -np.sin(ang)
    f_im[0, :] = np.cos(np.pi * n)
    g_re = (2.0 / n2) * np.cos(ang).T
    g_re[:, 0] = 1.0 / n2
    g_im = -(2.0 / n2) * np.sin(ang).T
    g_im[:, 0] = np.cos(np.pi * n) / n2
    kt = l // tk
    fwd = np.stack([f_re.reshape(kt, tk, l), f_im.reshape(kt, tk, l)], axis=1).reshape(2 * l, l)
    inv = np.stack([g_re.reshape(l, kt, tk), g_im.reshape(l, kt, tk)], axis=2).reshape(l, 2 * l)
    return np.asarray(fwd, dtype=BF16), np.asarray(inv, dtype=BF16)


def _hy_filter_kernel(z_ref, t_ref, dl_ref, w1_ref, b1_ref, w2_ref, b2_ref, w3_ref, b3_ref,
                      fr_ref, w4_ref, o_ref, *, half):
    hp = lax.Precision.HIGHEST
    fr = fr_ref[...]
    h = jnp.sin(fr * (jnp.dot(z_ref[...], w1_ref[...], precision=hp, preferred_element_type=F32) + b1_ref[...]))
    h = jnp.sin(fr * (jnp.dot(h, w2_ref[...], precision=hp, preferred_element_type=F32) + b2_ref[...]))
    h = jnp.sin(fr * (jnp.dot(h, w3_ref[...], precision=hp, preferred_element_type=F32) + b3_ref[...]))
    h = jnp.dot(h, w4_ref[...], precision=hp, preferred_element_type=F32)
    h = h * jnp.exp(-t_ref[...] * dl_ref[...])
    tl, nc = h.shape
    row = lax.broadcasted_iota(jnp.int32, (tl, nc), 0) + pl.program_id(0) * tl
    col = lax.broadcasted_iota(jnp.int32, (tl, nc), 1)
    h = jnp.where((row == 0) & (col >= half), 0.0, h)
    o_ref[...] = h.astype(o_ref.dtype)


def _pad2(a, r, c):
    return jnp.pad(a.astype(F32), ((0, r - a.shape[0]), (0, c - a.shape[1])))


def _hy_filters(l, w1, b1, w2, b2, w3, b3, freq, w4):
    t = jnp.linspace(0.0, 1.0, l, dtype=F32)[:, None]
    bands = (HY_POS_DIM - 1) // 2
    w = 2.0 * math.pi * jnp.arange(l, dtype=F32)[:, None] / l
    f = jnp.linspace(1e-4, bands - 1, bands, dtype=F32)[None, :]
    z = jnp.concatenate([t, jnp.cos(f * w), -jnp.sin(f * w)], axis=-1)
    nc = w4.shape[1]
    max_decay = math.log(HY_DECAY_TARGET) / HY_FAST_DECAY
    min_decay = math.log(HY_DECAY_TARGET) / HY_SLOW_DECAY
    deltas = jnp.abs(jnp.linspace(min_decay, max_decay, nc, dtype=F32))[None, :]
    hid = LANES
    tl = min(256, l)
    full = lambda r, c: pl.BlockSpec((r, c), lambda i: (0, 0))
    return pl.pallas_call(
        functools.partial(_hy_filter_kernel, half=nc // 2),
        grid=(l // tl,),
        in_specs=[pl.BlockSpec((tl, hid), lambda i: (i, 0)), pl.BlockSpec((tl, 1), lambda i: (i, 0)),
                  full(1, nc), full(hid, hid), full(1, hid), full(hid, hid), full(1, hid),
                  full(hid, hid), full(1, hid), full(1, hid), full(hid, nc)],
        out_specs=pl.BlockSpec((tl, nc), lambda i: (i, 0)),
        out_shape=jax.ShapeDtypeStruct((l, nc), BF16),
        compiler_params=_cparams(1, 32),
        name="hy_filter",
    )(_pad2(z, l, hid), t, deltas, _pad2(w1, hid, hid), _pad2(b1[None], 1, hid),
      _pad2(w2, hid, hid), _pad2(b2[None], 1, hid), _pad2(w3, hid, hid), _pad2(b3[None], 1, hid),
      _pad2(freq[None], 1, hid), _pad2(w4, hid, nc))


def _spectrum_kernel(f_ref, tf_ref, tb_ref, o_ref, *, tk):
    sf = jnp.dot(f_ref[...], tf_ref[...], preferred_element_type=F32)
    sb = jnp.dot(f_ref[...], tb_ref[...], preferred_element_type=F32)
    im = sf[tk:] - sb[tk:]
    nyq = (lax.broadcasted_iota(jnp.int32, im.shape, 0) == 0) & (pl.program_id(0) == 0)
    o_ref[0:tk, :] = sf[:tk] + sb[:tk]
    o_ref[tk:2 * tk, :] = jnp.where(nyq, sf[tk:] + sb[tk:], im)


def _hy_spectrum(fwd, taps, tk):
    l2, l = fwd.shape
    c = taps.shape[1] // (2 * HY_ORDER)
    return pl.pallas_call(
        functools.partial(_spectrum_kernel, tk=tk),
        grid=(l2 // (2 * tk), HY_ORDER),
        in_specs=[pl.BlockSpec((2 * tk, l), lambda ki, o: (ki, 0)),
                  pl.BlockSpec((l, c), lambda ki, o: (0, o)),
                  pl.BlockSpec((l, c), lambda ki, o: (0, HY_ORDER + o))],
        out_specs=pl.BlockSpec((2 * tk, c), lambda ki, o: (ki, o)),
        out_shape=jax.ShapeDtypeStruct((l2, HY_ORDER * c), F32),
        compiler_params=_cparams(2, 40),
        name="hy_spectrum",
    )(fwd, taps, taps)


def _short_conv3(u_ref, w_ref, b_ref, o_ref):
    l, c = o_ref.shape
    row = lax.broadcasted_iota(jnp.int32, (l, CONV_CB), 0)
    for c0 in range(0, c, CONV_CB):
        u = u_ref[0, :, c0:c0 + CONV_CB].astype(F32)
        prev = jnp.where(row == 0, 0.0, pltpu.roll(u, 1, 0))
        nxt = jnp.where(row == l - 1, 0.0, pltpu.roll(u, l - 1, 0))
        w = w_ref[:, c0:c0 + CONV_CB]
        out = w[0:1] * prev + w[1:2] * u + w[2:3] * nxt + b_ref[:, c0:c0 + CONV_CB]
        o_ref[:, c0:c0 + CONV_CB] = out.astype(o_ref.dtype)


def _long_conv_kernel(x_ref, gate_ref, xw_ref, xb_ref, gw_ref, gb_ref, skip_ref, f_ref, g_ref, h_ref,
                      o_ref, acc_ref, xs_ref, gs_ref, *, tk, conv_x):
    kt = pl.program_id(1)

    @pl.when(kt == 0)
    def _():
        acc_ref[...] = jnp.zeros_like(acc_ref)
        if conv_x:
            _short_conv3(x_ref, xw_ref, xb_ref, xs_ref)
        else:
            xs_ref[...] = x_ref[0]
        _short_conv3(gate_ref, gw_ref, gb_ref, gs_ref)

    z = jnp.dot(f_ref[...], xs_ref[...], preferred_element_type=F32)
    zr, zi = z[:tk], z[tk:]
    hr, hi = h_ref[0:tk, :], h_ref[tk:2 * tk, :]
    nyq = (lax.broadcasted_iota(jnp.int32, zr.shape, 0) == 0) & (kt == 0)
    yr = zr * hr - jnp.where(nyq, 0.0, zi * hi)
    yi = jnp.where(nyq, zi * hi, zr * hi + zi * hr)
    y = jnp.concatenate([yr, yi], axis=0).astype(BF16)
    acc_ref[...] += jnp.dot(g_ref[...], y, preferred_element_type=F32)

    @pl.when(kt == pl.num_programs(1) - 1)
    def _():
        o_ref[0] = (gs_ref[...].astype(F32) * (acc_ref[...] + xs_ref[...].astype(F32) * skip_ref[...])
                    ).astype(o_ref.dtype)


def _long_conv(x_arr, x_blk, conv_x, u, gate_blk, conv_w, conv_b, skip, spec, order, fwd, inv, tk):
    b, l, _ = x_arr.shape
    c = skip.shape[-1]
    kt = l // tk
    once = pl.Buffered(1)
    xcol = x_blk if conv_x else 0
    return pl.pallas_call(
        functools.partial(_long_conv_kernel, tk=tk, conv_x=conv_x),
        grid=(b, kt),
        in_specs=[pl.BlockSpec((1, l, c), lambda bi, ki: (bi, 0, x_blk), pipeline_mode=once),
                  pl.BlockSpec((1, l, c), lambda bi, ki: (bi, 0, gate_blk), pipeline_mode=once),
                  pl.BlockSpec((3, c), lambda bi, ki: (0, xcol)),
                  pl.BlockSpec((1, c), lambda bi, ki: (0, xcol)),
                  pl.BlockSpec((3, c), lambda bi, ki: (0, gate_blk)),
                  pl.BlockSpec((1, c), lambda bi, ki: (0, gate_blk)),
                  pl.BlockSpec((1, c), lambda bi, ki: (0, 0)),
                  pl.BlockSpec((2 * tk, l), lambda bi, ki: (ki, 0)),
                  pl.BlockSpec((l, 2 * tk), lambda bi, ki: (0, ki)),
                  pl.BlockSpec((2 * tk, c), lambda bi, ki: (ki, order))],
        out_specs=pl.BlockSpec((1, l, c), lambda bi, ki: (bi, 0, 0)),
        out_shape=jax.ShapeDtypeStruct((b, l, c), BF16),
        scratch_shapes=[pltpu.VMEM((l, c), F32), pltpu.VMEM((l, c), BF16), pltpu.VMEM((l, c), BF16)],
        compiler_params=_cparams(2, 56),
        name=f"hy_long_conv{order}",
    )(x_arr, u, conv_w, conv_b, conv_w, conv_b, skip.astype(F32).reshape(1, c), fwd, inv, spec)


def _cf_conv_kernel(a_ref, g_ref, w_ref, b_ref, o_ref, zpad_ref, zsh_ref, *, chunk):
    l = a_ref.shape[1]
    pad = 16
    sub = 8
    z = a_ref[0] * _sigmoid(g_ref[0])
    zeros = jnp.zeros((pad, z.shape[1]), F32)
    zpad_ref[0:pad, :] = zeros
    zpad_ref[pad:pad + l, :] = z
    zpad_ref[pad + l:pad + l + pad, :] = zeros
    half = CF_K // 2
    nsh = zsh_ref.shape[0]
    for s in range(sub):
        zsh_ref[...] = zpad_ref[s:s + nsh, :]
        taps = [k for k in range(CF_K) if (pad - half + k) % sub == s]
        for c in range(l // chunk):
            if s == 0:
                acc = jnp.broadcast_to(b_ref[...], (chunk, z.shape[1]))
            else:
                acc = o_ref[0, c * chunk:(c + 1) * chunk, :]
            for k in taps:
                r0 = c * chunk + (pad - half + k) - s
                acc = acc + w_ref[k:k + 1, :] * zsh_ref[r0:r0 + chunk, :]
            o_ref[0, c * chunk:(c + 1) * chunk, :] = acc


def _cf_conv(cf_in, w, bias):
    b, l, w2 = cf_in.shape
    wd = w2 // 2
    cb = CONV_CB
    nb = wd // cb
    chunk = min(256, l)
    return pl.pallas_call(
        functools.partial(_cf_conv_kernel, chunk=chunk),
        grid=(b, nb),
        in_specs=[pl.BlockSpec((1, l, cb), lambda bi, ci: (bi, 0, ci)),
                  pl.BlockSpec((1, l, cb), lambda bi, ci: (bi, 0, ci + nb)),
                  pl.BlockSpec((CF_K, cb), lambda bi, ci: (0, ci)),
                  pl.BlockSpec((1, cb), lambda bi, ci: (0, ci))],
        out_specs=pl.BlockSpec((1, l, cb), lambda bi, ci: (bi, 0, ci)),
        out_shape=jax.ShapeDtypeStruct((b, l, wd), F32),
        scratch_shapes=[pltpu.VMEM((l + 32, cb), F32), pltpu.VMEM((l + 24, cb), F32)],
        compiler_params=_cparams(2, 32),
        name="cf_conv",
    )(cf_in, cf_in, w.astype(F32), bias.astype(F32).reshape(1, wd))


def _merge_kernel(ya_ref, yh_ref, yc_ref, ga_ref, gh_ref, gc_ref, h_ref, wa_ref, wh_ref, wc_ref,
                  wo_ref, bo_ref, cg_ref, cb_ref, g1_ref, b1_ref, o_ref, *, alpha):
    yc = _layer_norm(yc_ref[...], cg_ref[...], cb_ref[...])
    yc = (yc * _sigmoid(yc)).astype(BF16)
    m = _sigmoid(ga_ref[...].astype(F32)) * jnp.dot(ya_ref[...], wa_ref[...], preferred_element_type=F32)
    m = m + _sigmoid(gh_ref[...].astype(F32)) * jnp.dot(yh_ref[...], wh_ref[...], preferred_element_type=F32)
    m = m + _sigmoid(gc_ref[...].astype(F32)) * jnp.dot(yc, wc_ref[...], preferred_element_type=F32)
    mix = jnp.dot(m.astype(BF16), wo_ref[...], preferred_element_type=F32) + bo_ref[...]
    o_ref[...] = _layer_norm(alpha * h_ref[...] + mix, g1_ref[...], b1_ref[...])


def _merge(ya, yh, yc, gates, h, wa, wh, wc, wo, bo, cg, cb, g1, b1, layer, alpha):
    n, d = h.shape
    w = ya.shape[1]
    tm = ROW_TILE
    row = lambda c: pl.BlockSpec((tm, c), lambda i: (i, 0))
    gate = lambda j: pl.BlockSpec((tm, d), lambda i: (i, j))
    once = pl.Buffered(1)
    wspec = lambda r, c: pl.BlockSpec((None, r, c), lambda i: (layer, 0, 0), pipeline_mode=once)
    vec = lambda c: pl.BlockSpec((None, 1, c), lambda i: (layer, 0, 0))
    return pl.pallas_call(
        functools.partial(_merge_kernel, alpha=alpha),
        grid=(n // tm,),
        in_specs=[row(w), row(w), row(w), gate(0), gate(1), gate(2), row(d),
                  wspec(w, d), wspec(w, d), wspec(w, d), wspec(d, d),
                  vec(d), vec(w), vec(w), vec(d), vec(d)],
        out_specs=row(d),
        out_shape=jax.ShapeDtypeStruct((n, d), F32),
        compiler_params=_cparams(1, 48),
        name="merge",
    )(ya, yh, yc, gates, gates, gates, h, wa, wh, wc, wo, bo, cg, cb, g1, b1)


def _first_max(vals):
    m = vals[0]
    for v in vals[1:]:
        m = jnp.maximum(m, v)
    idx = jnp.full(m.shape, len(vals) - 1, jnp.int32)
    for j in range(len(vals) - 2, -1, -1):
        idx = jnp.where(vals[j] == m, j, idx)
    return m, idx


def _top2(vals):
    m1, i1 = _first_max(vals)
    rest = [jnp.where(i1 == j, -1.0, v) for j, v in enumerate(vals)]
    m2, i2 = _first_max(rest)
    return m1, i1, m2, i2


def _router_kernel(h_ref, wr_ref, br_ref, idx_ref, wgt_ref, cnt_ref, carry_ref):
    i = pl.program_id(0)

    @pl.when(i == 0)
    def _():
        carry_ref[...] = jnp.zeros_like(carry_ref)

    h = h_ref[...]
    h_hi = h.astype(BF16)
    h_lo = (h - h_hi.astype(F32)).astype(BF16)
    w = wr_ref[...]
    w_hi = w.astype(BF16)
    w_lo = (w - w_hi.astype(F32)).astype(BF16)
    logits = (jnp.dot(h_hi, w_hi, preferred_element_type=F32) + jnp.dot(h_hi, w_lo, preferred_element_type=F32)
              + jnp.dot(h_lo, w_hi, preferred_element_type=F32)) + br_ref[...]
    lt = logits.T[:N_EXPERTS]
    tm = lt.shape[1]
    mx = jnp.max(lt, axis=0, keepdims=True)
    ex = jnp.exp(lt - mx)
    probs = ex / jnp.sum(ex, axis=0, keepdims=True)
    p = [probs[e:e + 1, :] for e in range(N_EXPERTS)]
    scores = []
    for g in range(N_GROUPS):
        a, _, b, _ = _top2(p[g * EXPERTS_PER_GROUP:(g + 1) * EXPERTS_PER_GROUP])
        scores.append(a + b)
    _, g_sel = _first_max(scores)
    pg = []
    for j in range(EXPERTS_PER_GROUP):
        v = p[(N_GROUPS - 1) * EXPERTS_PER_GROUP + j]
        for g in range(N_GROUPS - 2, -1, -1):
            v = jnp.where(g_sel == g, p[g * EXPERTS_PER_GROUP + j], v)
        pg.append(v)
    p1, i1, p2, i2 = _top2(pg)
    den = p1 + p2
    e0 = g_sel * EXPERTS_PER_GROUP + i1
    e1 = g_sel * EXPERTS_PER_GROUP + i2

    erow = lax.broadcasted_iota(jnp.int32, (N_EXPERTS, tm), 0)
    oh0 = (erow == e0).astype(F32)
    oh1 = (erow == e1).astype(F32)
    both = oh0 + oh1
    before = (lax.broadcasted_iota(jnp.int32, (tm, tm), 0) < lax.broadcasted_iota(jnp.int32, (tm, tm), 1))
    cum = jnp.dot(both.astype(BF16), before.astype(BF16), preferred_element_type=F32) + carry_ref[:, 0:1]
    r0 = jnp.sum(oh0 * cum, axis=0, keepdims=True)
    r1 = jnp.sum(oh1 * cum, axis=0, keepdims=True)
    carry_ref[...] = carry_ref[...] + jnp.sum(both, axis=1, keepdims=True)
    cnt_ref[...] = carry_ref[...]

    zi = jnp.zeros((4, tm), jnp.int32)
    idx_ref[...] = jnp.concatenate([e0, e1, r0.astype(jnp.int32), r1.astype(jnp.int32), zi], axis=0)
    zf = jnp.zeros((6, tm), F32)
    wgt_ref[...] = jnp.concatenate([p1 / den, p2 / den, zf], axis=0)


def _router(h, w_router, b_router):
    n, d = h.shape
    tm = ROW_TILE
    wr = _pad2(w_router, d, LANES)
    br = _pad2(b_router[None], 1, LANES)
    return pl.pallas_call(
        _router_kernel,
        grid=(n // tm,),
        in_specs=[pl.BlockSpec((tm, d), lambda i: (i, 0)),
                  pl.BlockSpec((d, LANES), lambda i: (0, 0)),
                  pl.BlockSpec((1, LANES), lambda i: (0, 0))],
        out_specs=[pl.BlockSpec((8, tm), lambda i: (0, i)),
                   pl.BlockSpec((8, tm), lambda i: (0, i)),
                   pl.BlockSpec((N_EXPERTS, LANES), lambda i: (0, 0))],
        out_shape=[jax.ShapeDtypeStruct((8, n), jnp.int32),
                   jax.ShapeDtypeStruct((8, n), F32),
                   jax.ShapeDtypeStruct((N_EXPERTS, LANES), F32)],
        scratch_shapes=[pltpu.VMEM((N_EXPERTS, LANES), F32)],
        compiler_params=_cparams(1, 32),
        name="router",
    )(h, wr, br)


def _expert_kernel(texp_ref, nv_ref, stok_ref, h_hbm, wg_ref, wu_ref, wd_ref, o_ref, xbuf, sem, *, tm):
    i = pl.program_id(0)
    nv = nv_ref[0]
    nbuf = xbuf.shape[0]

    def row_copy(tok, slot, r):
        return pltpu.make_async_copy(h_hbm.at[pl.ds(tok, 1), :], xbuf.at[slot, pl.ds(r, 1), :], sem.at[slot])

    def wait_tile(slot):
        pltpu.make_async_copy(h_hbm.at[pl.ds(0, tm), :], xbuf.at[slot], sem.at[slot]).wait()

    @pl.when(i == 0)
    def _():
        second = jnp.minimum(1, nv - 1) * tm

        def body(r, carry):
            row_copy(stok_ref[r], 0, r).start()
            row_copy(stok_ref[second + r], 1, r).start()
            return carry
        lax.fori_loop(0, tm, body, 0, unroll=GATHER_UNROLL)

    @pl.when(i < nv)
    def _():
        slot = i % nbuf
        wait_tile(slot)
        x = xbuf[slot].astype(BF16)
        g = jnp.dot(x, wg_ref[...], preferred_element_type=F32)
        u = jnp.dot(x, wu_ref[...], preferred_element_type=F32)
        hid = (g * _sigmoid(g) * u).astype(BF16)
        o_ref[...] = jnp.dot(hid, wd_ref[...], preferred_element_type=F32)
        base = jnp.minimum(i + 2, nv - 1) * tm
        nxt = (i + 2) % nbuf
        for r in range(tm):
            row_copy(stok_ref[base + r], nxt, r).start(priority=r % 2)

    @pl.when(i == nv - 1)
    def _():
        wait_tile(nv % nbuf)
        wait_tile((nv + 1) % nbuf)

    @pl.when(i >= nv)
    def _():
        o_ref[...] = jnp.zeros_like(o_ref)


def _experts(h, texp, nvalid, slot_tok, wg, wu, wd, layer):
    n, d = h.shape
    de = wg.shape[-1]
    tm = EXPERT_TM
    n_tiles = slot_tok.shape[0] // tm
    grid_spec = pltpu.PrefetchScalarGridSpec(
        num_scalar_prefetch=3,
        grid=(n_tiles,),
        in_specs=[pl.BlockSpec(memory_space=pl.ANY),
                  pl.BlockSpec((None, None, d, de), lambda i, te, nv, st: (layer, te[i], 0, 0)),
                  pl.BlockSpec((None, None, d, de), lambda i, te, nv, st: (layer, te[i], 0, 0)),
                  pl.BlockSpec((None, None, de, d), lambda i, te, nv, st: (layer, te[i], 0, 0))],
        out_specs=pl.BlockSpec((tm, d), lambda i, te, nv, st: (i, 0)),
        scratch_shapes=[pltpu.VMEM((3, tm, d), F32), pltpu.SemaphoreType.DMA((3,))],
    )
    return pl.pallas_call(
        functools.partial(_expert_kernel, tm=tm),
        grid_spec=grid_spec,
        out_shape=jax.ShapeDtypeStruct((n_tiles * tm, d), F32),
        compiler_params=_cparams(1, 48, row_gather=True),
        name="experts",
    )(texp, nvalid, slot_tok, h, wg, wu, wd)


def _combine_kernel(p0_ref, p1_ref, ys_hbm, h_ref, w0_ref, w1_ref, g_ref, b_ref, o_ref, ob_ref,
                    gbuf, sem, *, tm, alpha):
    i = pl.program_id(0)
    nt = pl.num_programs(0)
    nbuf = gbuf.shape[0]

    def start_rows(t, slot, r):
        pltpu.make_async_copy(ys_hbm.at[pl.ds(p0_ref[t], 1), :], gbuf.at[slot, 0, pl.ds(r, 1), :],
                              sem.at[slot]).start(priority=0)
        pltpu.make_async_copy(ys_hbm.at[pl.ds(p1_ref[t], 1), :], gbuf.at[slot, 1, pl.ds(r, 1), :],
                              sem.at[slot]).start(priority=1)

    def wait_tile(slot):
        for j in range(2):
            pltpu.make_async_copy(ys_hbm.at[pl.ds(0, tm), :], gbuf.at[slot, j], sem.at[slot]).wait()

    @pl.when(i == 0)
    def _():
        second = jnp.minimum(1, nt - 1) * tm

        def body(r, carry):
            start_rows(r, 0, r)
            start_rows(second + r, 1, r)
            return carry
        lax.fori_loop(0, tm, body, 0, unroll=GATHER_UNROLL)

    slot = i % nbuf
    wait_tile(slot)
    y = w0_ref[...] * gbuf[slot, 0] + w1_ref[...] * gbuf[slot, 1]
    out = _layer_norm(alpha * h_ref[...] + y, g_ref[...], b_ref[...])
    o_ref[...] = out
    ob_ref[...] = out.astype(BF16)
    base = jnp.minimum(i + 2, nt - 1) * tm
    nxt = (i + 2) % nbuf
    for r in range(tm):
        start_rows(base + r, nxt, r)

    @pl.when(i == nt - 1)
    def _():
        wait_tile(nt % nbuf)
        wait_tile((nt + 1) % nbuf)


def _combine(ys, h, pos0, pos1, w0, w1, g2, b2, layer, alpha):
    n, d = h.shape
    tm = ROW_TILE
    row = pl.BlockSpec((tm, d), lambda i, a, b: (i, 0))
    col = pl.BlockSpec((tm, 1), lambda i, a, b: (i, 0))
    vec = pl.BlockSpec((None, 1, d), lambda i, a, b: (layer, 0, 0))
    grid_spec = pltpu.PrefetchScalarGridSpec(
        num_scalar_prefetch=2,
        grid=(n // tm,),
        in_specs=[pl.BlockSpec(memory_space=pl.ANY), row, col, col, vec, vec],
        out_specs=[row, row],
        scratch_shapes=[pltpu.VMEM((3, 2, tm, d), F32), pltpu.SemaphoreType.DMA((3,))],
    )
    return pl.pallas_call(
        functools.partial(_combine_kernel, tm=tm, alpha=alpha),
        grid_spec=grid_spec,
        out_shape=[jax.ShapeDtypeStruct((n, d), F32), jax.ShapeDtypeStruct((n, d), BF16)],
        compiler_params=_cparams(1, 44, row_gather=True),
        name="combine",
    )(pos0, pos1, ys, h, w0, w1, g2, b2)


def _slot_table_kernel(p0_ref, p1_ref, zeros_hbm, o_hbm, tab_ref, sem):
    n = p0_ref.shape[0]
    fill = pltpu.make_async_copy(zeros_hbm, tab_ref, sem)
    fill.start()
    fill.wait()

    def body(t, carry):
        tab_ref[p0_ref[t]] = t
        tab_ref[p1_ref[t]] = t
        return carry
    lax.fori_loop(0, n, body, 0, unroll=GATHER_UNROLL)
    out = pltpu.make_async_copy(tab_ref, o_hbm, sem)
    out.start()
    out.wait()


def _slot_table(pos0, pos1, n_slots):
    smem = pl.BlockSpec(memory_space=pltpu.SMEM)
    hbm = pl.BlockSpec(memory_space=pl.ANY)
    return pl.pallas_call(
        _slot_table_kernel,
        in_specs=[smem, smem, hbm],
        out_specs=hbm,
        out_shape=jax.ShapeDtypeStruct((n_slots,), jnp.int32),
        scratch_shapes=[pltpu.SMEM((n_slots,), jnp.int32), pltpu.SemaphoreType.DMA(())],
        name="slot_table",
    )(pos0, pos1, jnp.zeros((n_slots,), jnp.int32))


def _routing_tables(idx, cnt, n):
    tm = EXPERT_TM
    e0, e1, r0, r1 = idx[0], idx[1], idx[2], idx[3]
    counts = cnt[:, 0].astype(jnp.int32)
    padded = ((counts + tm - 1) // tm) * tm
    pend = jnp.cumsum(padded)
    poff = pend - padded
    eid = jnp.arange(N_EXPERTS, dtype=jnp.int32)[:, None]
    pos0 = jnp.sum(jnp.where(e0[None, :] == eid, poff[:, None], 0), axis=0) + r0
    pos1 = jnp.sum(jnp.where(e1[None, :] == eid, poff[:, None], 0), axis=0) + r1
    n_slots = 2 * n + N_EXPERTS * tm
    slot_tok = _slot_table(pos0, pos1, n_slots)
    n_tiles = n_slots // tm
    nvalid = pend[-1] // tm
    tile = jnp.arange(n_tiles, dtype=jnp.int32)
    start = jnp.minimum(tile, nvalid - 1) * tm
    texp = jnp.sum((start[:, None] >= pend[None, :]).astype(jnp.int32), axis=1)
    texp = jnp.minimum(texp, N_EXPERTS - 1)
    return pos0, pos1, slot_tok, texp, nvalid.reshape(1).astype(jnp.int32)


def kernel(x, in_ln_g, in_ln_b, w_in, b_in, attn_rpb, hy_conv_w, hy_conv_b, hy_f_w1, hy_f_b1, hy_f_w2, hy_f_b2, hy_f_w3, hy_f_b3, hy_f_freq, hy_f_w4, hy_skip, cf_dw_w, cf_dw_b, cf_ln_g, cf_ln_b, w_attn_br, w_hy_br, w_cf_br, w_o, b_o, ln1_g, ln1_b, w_router, b_router, moe_w_gate, moe_w_up, moe_w_down, ln2_g, ln2_b):
    bsz, l, d = x.shape
    depth = w_in.shape[0]
    n = bsz * l
    mw = w_attn_br.shape[1]
    alpha = (2 * depth) ** 0.25
    c_qkv, c_hy, c_cf, c_gate = 0, 3 * mw, 6 * mw, 8 * mw

    w_in_f = w_in.astype(F32)
    b_in3 = b_in.astype(F32)[:, None, :]
    wa_b, wh_b, wc_b, wo_b = (w.astype(BF16) for w in (w_attn_br, w_hy_br, w_cf_br, w_o))
    wg_b, wu_b, wd_b = (w.astype(BF16) for w in (moe_w_gate, moe_w_up, moe_w_down))
    vec3 = lambda v: v.astype(F32)[:, None, :]
    bo3, cg3, cb3, g13, b13, g23, b23 = map(vec3, (b_o, cf_ln_g, cf_ln_b, ln1_g, ln1_b, ln2_g, ln2_b))
    tk = min(DFT_TK, l)
    fwd_np, inv_np = _dft_matrices(l, tk)
    fwd, inv = jnp.asarray(fwd_np), jnp.asarray(inv_np)

    h, hb = _in_ln(x.reshape(n, d), in_ln_g, in_ln_b)
    for layer in range(depth):
        proj = lambda c0, nc, dt, nm: _proj(hb, w_in_f, b_in3, layer=layer, col0=c0, ncols=nc,
                                            out_dtype=dt, name=nm)
        qkv = proj(c_qkv, 3 * mw, BF16, "proj_qkv")
        hy_in = proj(c_hy, 3 * mw, BF16, "proj_hyena")
        cf_in = proj(c_cf, 2 * mw, F32, "proj_conformer")
        gates = proj(c_gate, 3 * d, BF16, "proj_gates")

        y_a = _natten(qkv.reshape(bsz, l, 3 * mw), _natten_bias_table(attn_rpb[layer]))

        taps = _hy_filters(l, hy_f_w1[layer], hy_f_b1[layer], hy_f_w2[layer], hy_f_b2[layer],
                           hy_f_w3[layer], hy_f_b3[layer], hy_f_freq[layer], hy_f_w4[layer])
        spec = _hy_spectrum(fwd, taps, tk)
        u = hy_in.reshape(bsz, l, 3 * mw)
        cw, cb_ = hy_conv_w[layer].astype(F32), hy_conv_b[layer].astype(F32)[None]
        z1 = _long_conv(u, 0, True, u, 1, cw, cb_, hy_skip[layer, 0], spec, 0, fwd, inv, tk)
        y_h = _long_conv(z1, 0, False, u, 2, cw, cb_, hy_skip[layer, 1], spec, 1, fwd, inv, tk)

        y_c = _cf_conv(cf_in.reshape(bsz, l, 2 * mw), cf_dw_w[layer], cf_dw_b[layer])

        h = _merge(y_a.reshape(n, mw), y_h.reshape(n, mw), y_c.reshape(n, mw), gates, h,
                   wa_b, wh_b, wc_b, wo_b, bo3, cg3, cb3, g13, b13, layer, alpha)

        idx, wgt, cnt = _router(h, w_router, b_router)
        pos0, pos1, slot_tok, texp, nvalid = _routing_tables(idx, cnt, n)
        ys = _experts(h, texp, nvalid, slot_tok, wg_b, wu_b, wd_b, layer)
        h, hb = _combine(ys, h, pos0, pos1, wgt[0].reshape(n, 1), wgt[1].reshape(n, 1),
                         g23, b23, layer, alpha)
    return h.reshape(bsz, l, d)
```
